```python
import jax, jax.numpy as jnp
from jax import lax
import numpy as np

D_MODEL = 1024
BATCH = 8
SEQ = 4096
DEPTH = 4

N_MIXERS = 4
GROUP_WIDTH = D_MODEL // N_MIXERS
HEAD_DIM = 64
N_GROUP_HEADS = GROUP_WIDTH // HEAD_DIM
D_FF = 4 * D_MODEL
N_MOD = 6
RMS_EPS = 1e-6
NEG_INF = -1e30

ROPE_THETA = 500000.0
ROPE_DIM = HEAD_DIM // 4

CONV_WIDTH = 3

CMP_BLOCK = 32
CMP_STRIDE = 16
CMP_HIDDEN = 2 * HEAD_DIM
SLC_BLOCK = 64
N_SLC = 16
N_INIT_BLOCKS = 1
N_LOCAL_BLOCKS = 2
FORCE_BONUS = 1e4
WINDOW = 512
Q_BLOCK = 128
N_BRANCH = 3

GLA_DK = HEAD_DIM // 2
GLA_DV = HEAD_DIM
GLA_RANK = 16
GLA_TAU = 16.0
GLA_CHUNK = 64

GMLP_CHUNK = 128
GMLP_DIM = GROUP_WIDTH // N_GROUP_HEADS

SPLIT_SIZES = (
    GROUP_WIDTH, GROUP_WIDTH, GROUP_WIDTH,
    GROUP_WIDTH, 2 * N_BRANCH * HEAD_DIM, N_BRANCH * N_GROUP_HEADS,
    N_GROUP_HEADS * GLA_DK, N_GROUP_HEADS * GLA_DK,
    N_GROUP_HEADS * GLA_DV, N_GROUP_HEADS * GLA_DV, GLA_RANK,
    GROUP_WIDTH, GROUP_WIDTH,
)
D_IN_PROJ = sum(SPLIT_SIZES)

kernel_name = 'hybrid_parallel_group_block'


def rms_norm(x, gain=None):
    xf = x.astype(jnp.float32)
    y = xf * lax.rsqrt(jnp.mean(xf * xf, axis=-1, keepdims=True) + RMS_EPS)
    if gain is not None:
        y = y * gain.astype(jnp.float32)
    return y.astype(x.dtype)


def layer_norm(x):
    xf = x.astype(jnp.float32)
    mu = jnp.mean(xf, axis=-1, keepdims=True)
    var = jnp.mean(jnp.square(xf - mu), axis=-1, keepdims=True)
    return ((xf - mu) * lax.rsqrt(var + RMS_EPS)).astype(x.dtype)


def split_cols(z, sizes):
    out, start = [], 0
    for s in sizes:
        out.append(z[..., start:start + s])
        start += s
    return out


def partial_rotary(x, positions):
    half = ROPE_DIM // 2
    inv_freq = ROPE_THETA ** (-jnp.arange(0, ROPE_DIM, 2, dtype=jnp.float32) / ROPE_DIM)
    ang = positions.astype(jnp.float32)[..., None] * inv_freq
    cos = jnp.cos(ang)[:, :, None, :]
    sin = jnp.sin(ang)[:, :, None, :]
    xr = x[..., :ROPE_DIM].astype(jnp.float32)
    x1, x2 = xr[..., :half], xr[..., half:]
    rot = jnp.concatenate([x1 * cos - x2 * sin, x2 * cos + x1 * sin], axis=-1).astype(x.dtype)
    return jnp.concatenate([rot, x[..., ROPE_DIM:]], axis=-1)


def masked_softmax(s, mask):
    p = jax.nn.softmax(jnp.where(mask, s, NEG_INF), axis=-1)
    return jnp.where(mask, p, 0.0)


def short_conv_mixer(h_in, b_gate, c_gate, conv_w):
    z = c_gate * h_in
    y = lax.conv_general_dilated(z, conv_w.astype(z.dtype)[:, None, :], (1,), ((CONV_WIDTH - 1, 0),),
                                 dimension_numbers=('NWC', 'WIO', 'NWC'),
                                 feature_group_count=GROUP_WIDTH)
    return b_gate * y


def cmp_to_slc_matrix(n_cmp, n_blk):
    cs = np.arange(n_cmp) * CMP_STRIDE
    ss = np.arange(n_blk) * SLC_BLOCK
    ov = np.minimum(cs[:, None] + CMP_BLOCK, ss[None, :] + SLC_BLOCK) - np.maximum(cs[:, None], ss[None, :])
    return jnp.asarray(np.clip(ov, 0, None) / CMP_STRIDE, dtype=jnp.float32)


def nsa_mixer(q_in, kv_in, gate_in, positions, cmp_pos, cmp_w1, cmp_w2):
    B, S, _ = q_in.shape
    H, HD = N_GROUP_HEADS, HEAD_DIM
    scale = HD ** -0.5
    q = q_in.reshape(B, S, H, HD)
    k_cmp, v_cmp, k_slc, v_slc, k_win, v_win = jnp.split(kv_in, 2 * N_BRANCH, axis=-1)

    n_cmp = (S - CMP_BLOCK) // CMP_STRIDE + 1
    blk = np.arange(n_cmp)[:, None] * CMP_STRIDE + np.arange(CMP_BLOCK)[None, :]

    def compress(t, j):
        z = (t[:, blk] + cmp_pos[j]).reshape(B, n_cmp, CMP_BLOCK * HD)
        return jax.nn.silu(z @ cmp_w1[j]) @ cmp_w2[j]

    kc, vc = compress(k_cmp, 0), compress(v_cmp, 1)
    t_idx = jnp.arange(S)
    cmp_end = jnp.arange(n_cmp) * CMP_STRIDE + CMP_BLOCK - 1
    mask_c = cmp_end[None, :] <= t_idx[:, None]
    s_c = jnp.einsum('bshd,bnd->bhsn', q, kc).astype(jnp.float32) * scale
    p_c = masked_softmax(s_c, mask_c)
    o_cmp = jnp.einsum('bhsn,bnd->bshd', p_c.astype(vc.dtype), vc)

    n_blk = S // SLC_BLOCK
    p_slc = jnp.einsum('bhsn,nj->bsj', p_c, cmp_to_slc_matrix(n_cmp, n_blk))
    cur = (t_idx // SLC_BLOCK)[:, None]
    j = jnp.arange(n_blk)[None, :]
    forced = (j < N_INIT_BLOCKS) | ((cur - j >= 0) & (cur - j < N_LOCAL_BLOCKS))
    score = jnp.where(j <= cur, p_slc + jnp.where(forced, FORCE_BONUS, 0.0), NEG_INF)
    n_top = min(N_SLC, n_blk)
    _, sel_idx = lax.top_k(score, n_top)

    q_r = partial_rotary(q, positions)
    rope_k = lambda t: partial_rotary(t[:, :, None, :], positions)[:, :, 0, :]
    k_s_blk = rope_k(k_slc).reshape(B, n_blk, SLC_BLOCK, HD)
    v_s_blk = v_slc.reshape(B, n_blk, SLC_BLOCK, HD)
    pad = ((0, 0), (WINDOW, 0), (0, 0))
    k_w_pad = jnp.pad(rope_k(k_win), pad)
    v_w_pad = jnp.pad(v_win, pad)

    n_qb = S // Q_BLOCK
    q_blocks = q_r.reshape(B, n_qb, Q_BLOCK, H, HD).transpose(1, 0, 3, 2, 4)
    idx_blocks = sel_idx.reshape(B, n_qb, Q_BLOCK, n_top).transpose(1, 0, 2, 3)
    take = jax.vmap(lambda blocks, idx: blocks[idx])

    def block_fn(args):
        qb, q_b, idx_b = args
        tq = qb * Q_BLOCK + jnp.arange(Q_BLOCK)
        k_sel = take(k_s_blk, idx_b).reshape(B, Q_BLOCK, n_top * SLC_BLOCK, HD)
        v_sel = take(v_s_blk, idx_b).reshape(B, Q_BLOCK, n_top * SLC_BLOCK, HD)
        kpos = (idx_b[..., None] * SLC_BLOCK + jnp.arange(SLC_BLOCK)).reshape(B, Q_BLOCK, n_top * SLC_BLOCK)
        s_s = jnp.einsum('bhqd,bqkd->bhqk', q_b, k_sel).astype(jnp.float32) * scale
        p_s = masked_softmax(s_s, (kpos <= tq[None, :, None])[:, None])
        o_s = jnp.einsum('bhqk,bqkd->bhqd', p_s.astype(v_sel.dtype), v_sel)
        start = qb * Q_BLOCK
        k_wb = lax.dynamic_slice_in_dim(k_w_pad, start, Q_BLOCK + WINDOW, axis=1)
        v_wb = lax.dynamic_slice_in_dim(v_w_pad, start, Q_BLOCK + WINDOW, axis=1)
        kpos_w = start - WINDOW + jnp.arange(Q_BLOCK + WINDOW)
        diff = tq[:, None] - kpos_w[None, :]
        m_w = (diff >= 0) & (diff < WINDOW) & (kpos_w[None, :] >= 0)
        s_w = jnp.einsum('bhqd,bkd->bhqk', q_b, k_wb).astype(jnp.float32) * scale
        p_w = masked_softmax(s_w, m_w)
        o_w = jnp.einsum('bhqk,bkd->bhqd', p_w.astype(v_wb.dtype), v_wb)
        return o_s, o_w

    o_sel, o_win = lax.map(block_fn, (jnp.arange(n_qb), q_blocks, idx_blocks))
    to_bshd = lambda o: o.transpose(1, 0, 3, 2, 4).reshape(B, S, H, HD)
    g = jax.nn.sigmoid(gate_in.astype(jnp.float32)).astype(q.dtype).reshape(B, S, H, N_BRANCH)
    o = g[..., 0:1] * o_cmp + g[..., 1:2] * to_bshd(o_sel) + g[..., 2:3] * to_bshd(o_win)
    return o.reshape(B, S, GROUP_WIDTH)


def gla_mixer(q_in, k_in, v_in, g_in, lr_in, gate_w2, gate_b):
    B, S, _ = q_in.shape
    H, C = N_GROUP_HEADS, GLA_CHUNK
    n_c = S // C
    log_a = jax.nn.log_sigmoid((lr_in @ gate_w2 + gate_b).astype(jnp.float32)) / GLA_TAU

    def chunk(t, d):
        return t.reshape(B, n_c, C, H, d).transpose(0, 3, 1, 2, 4).astype(jnp.float32)

    q = chunk(q_in, GLA_DK) * GLA_DK ** -0.5
    k = chunk(k_in, GLA_DK)
    v = chunk(v_in, GLA_DV)
    b = jnp.cumsum(chunk(log_a, GLA_DK), axis=3)
    b_last = b[:, :, :, -1:, :]
    b_mid = b[:, :, :, C // 2:C // 2 + 1, :]
    causal = jnp.tril(jnp.ones((C, C), dtype=bool))
    att = jnp.einsum('bhntd,bhnsd->bhnts', q * jnp.exp(b - b_mid), k * jnp.exp(b_mid - b))
    o_intra = jnp.einsum('bhnts,bhnsv->bhntv', jnp.where(causal, att, 0.0), v)
    kv = jnp.einsum('bhnsd,bhnsv->bhndv', k * jnp.exp(b_last - b), v)
    decay = jnp.exp(b_last[:, :, :, 0, :])

    def step(state, inp):
        dec, kv_n = inp
        return dec[..., None] * state + kv_n, state

    init = jnp.zeros((B, H, GLA_DK, GLA_DV), jnp.float32)
    _, s_prev = lax.scan(step, init, (jnp.moveaxis(decay, 2, 0), jnp.moveaxis(kv, 2, 0)))
    o_inter = jnp.einsum('bhntd,nbhdv->bhntv', q * jnp.exp(b), s_prev)
    o = (o_intra + o_inter).transpose(0, 2, 3, 1, 4).reshape(B, S, H, GLA_DV)
    g = jax.nn.silu(g_in.astype(jnp.float32)).reshape(B, S, H, GLA_DV)
    return (g * rms_norm(o)).reshape(B, S, GROUP_WIDTH).astype(q_in.dtype)


def gmlp_mixer(u_in, v_in, w_s, b_s):
    B, S, _ = u_in.shape
    G, T = N_GROUP_HEADS, GMLP_CHUNK
    n_ch = S // T
    v = layer_norm(v_in.reshape(B, S, G, GMLP_DIM)).reshape(B, n_ch, T, G, GMLP_DIM)
    w = jnp.where(jnp.tril(jnp.ones((T, T), dtype=bool)), w_s, 0.0)
    mixed = jnp.einsum('gts,bnsgc->bntgc', w, v) + b_s.T[:, :, None]
    return (u_in.reshape(B, n_ch, T, G, GMLP_DIM) * mixed).reshape(B, S, GROUP_WIDTH)


def mixer_block(h, positions, w_in, conv_w, cmp_pos, cmp_w1, cmp_w2, gla_gate_w2, gla_gate_b,
                gmlp_ws, gmlp_b, grp_gain, w_o):
    B, S, _ = h.shape
    (a_h, a_b, a_c, n_q, n_kv, n_g, l_q, l_k, l_v, l_g, l_lr, m_u, m_v) = split_cols(h @ w_in, SPLIT_SIZES)
    outs = jnp.stack([
        short_conv_mixer(a_h, a_b, a_c, conv_w),
        nsa_mixer(n_q, n_kv, n_g, positions, cmp_pos, cmp_w1, cmp_w2),
        gla_mixer(l_q, l_k, l_v, l_g, l_lr, gla_gate_w2, gla_gate_b),
        gmlp_mixer(m_u, m_v, gmlp_ws, gmlp_b),
    ], axis=2)
    outs = rms_norm(outs, grp_gain.reshape(N_MIXERS, GROUP_WIDTH))
    return outs.reshape(B, S, D_MODEL) @ w_o


def setup_inputs(seed: int = 0) -> dict:
    key = jax.random.key(seed)
    ks = jax.random.split(key, 20)
    nrm = lambda k, shape, s: jax.random.normal(k, shape, jnp.float32) * s
    positions = (jax.random.randint(ks[2], (BATCH, 1), 0, SEQ, dtype=jnp.int32)
                 + jnp.arange(SEQ, dtype=jnp.int32)[None, :])
    return {
        'x': nrm(ks[0], (BATCH, SEQ, D_MODEL), 1.0),
        'c': nrm(ks[1], (BATCH, D_MODEL), 1.0),
        'positions': positions,
        'w_in': nrm(ks[3], (DEPTH, D_MODEL, D_IN_PROJ), D_MODEL ** -0.5),
        'conv_w': nrm(ks[4], (DEPTH, CONV_WIDTH, GROUP_WIDTH), CONV_WIDTH ** -0.5),
        'cmp_pos': nrm(ks[5], (DEPTH, 2, CMP_BLOCK, HEAD_DIM), 0.1),
        'cmp_w1': nrm(ks[6], (DEPTH, 2, CMP_BLOCK * HEAD_DIM, CMP_HIDDEN), (CMP_BLOCK * HEAD_DIM) ** -0.5),
        'cmp_w2': nrm(ks[7], (DEPTH, 2, CMP_HIDDEN, HEAD_DIM), CMP_HIDDEN ** -0.5),
        'gla_gate_w2': nrm(ks[8], (DEPTH, GLA_RANK, N_GROUP_HEADS * GLA_DK), GLA_RANK ** -0.5),
        'gla_gate_b': nrm(ks[9], (DEPTH, N_GROUP_HEADS * GLA_DK), 0.1),
        'gmlp_ws': nrm(ks[10], (DEPTH, N_GROUP_HEADS, GMLP_CHUNK, GMLP_CHUNK), 0.5 * GMLP_CHUNK ** -0.5),
        'gmlp_b': 1.0 + nrm(ks[11], (DEPTH, N_GROUP_HEADS, GMLP_CHUNK), 0.1),
        'grp_gain': 1.0 + nrm(ks[12], (DEPTH, D_MODEL), 0.1),
        'w_o': nrm(ks[13], (DEPTH, D_MODEL, D_MODEL), D_MODEL ** -0.5),
        'norm_g': 1.0 + nrm(ks[14], (DEPTH, 4, D_MODEL), 0.1),
        'w_mod': nrm(ks[15], (DEPTH, D_MODEL, N_MOD * D_MODEL), D_MODEL ** -0.5),
        'b_mod': nrm(ks[16], (DEPTH, N_MOD * D_MODEL), 0.02),
        'w_up': nrm(ks[17], (DEPTH, D_MODEL, D_FF), D_MODEL ** -0.5),
        'w_down': nrm(ks[18], (DEPTH, D_FF, D_MODEL), D_FF ** -0.5),
    }


def reference(x, c, positions, w_in, conv_w, cmp_pos, cmp_w1, cmp_w2, gla_gate_w2, gla_gate_b,
              gmlp_ws, gmlp_b, grp_gain, w_o, norm_g, w_mod, b_mod, w_up, w_down):
    cond = jax.nn.silu(c)
    for l in range(DEPTH):
        mod = cond @ w_mod[l] + b_mod[l]
        sh1, sc1, gt1, sh2, sc2, gt2 = [m[:, None, :] for m in jnp.split(mod, N_MOD, axis=-1)]
        h = rms_norm(x, norm_g[l, 0]) * (1.0 + sc1) + sh1
        y = mixer_block(h, positions, w_in[l], conv_w[l], cmp_pos[l], cmp_w1[l], cmp_w2[l],
                        gla_gate_w2[l], gla_gate_b[l], gmlp_ws[l], gmlp_b[l], grp_gain[l], w_o[l])
        x = x + gt1 * rms_norm(y, norm_g[l, 1])
        h = rms_norm(x, norm_g[l, 2]) * (1.0 + sc2) + sh2
        y = jnp.square(jax.nn.relu(h @ w_up[l])) @ w_down[l]
        x = x + gt2 * rms_norm(y, norm_g[l, 3])
    return x
```

```python
import functools

import numpy as np
import jax
import jax.numpy as jnp
from jax import lax
from jax.experimental import pallas as pl
from jax.experimental.pallas import tpu as pltpu

D_MODEL = 1024
N_MIXERS = 4
GROUP_WIDTH = D_MODEL // N_MIXERS
HEAD_DIM = 64
N_GROUP_HEADS = GROUP_WIDTH // HEAD_DIM
D_FF = 4 * D_MODEL
N_MOD = 6
RMS_EPS = 1e-6
NEG_INF = -1e30

ROPE_THETA = 500000.0
ROPE_DIM = HEAD_DIM // 4
ROPE_HALF = ROPE_DIM // 2

CONV_WIDTH = 3

CMP_BLOCK = 32
CMP_STRIDE = 16
CMP_HIDDEN = 2 * HEAD_DIM
SLC_BLOCK = 64
N_SLC = 16
N_INIT_BLOCKS = 1
N_LOCAL_BLOCKS = 2
FORCE_BONUS = 1e4
WINDOW = 512
N_BRANCH = 3

GLA_DK = HEAD_DIM // 2
GLA_DV = HEAD_DIM
GLA_RANK = 16
GLA_TAU = 16.0
GLA_CHUNK = 64

GMLP_CHUNK = 128

LANE = 128
VMEM_LIMIT = 48 * 1024 * 1024

F32 = jnp.float32
BF16 = jnp.bfloat16

SEG_WIDTHS = (3 * GROUP_WIDTH,
              GROUP_WIDTH,
              2 * N_BRANCH * HEAD_DIM,
              LANE,
              2 * N_GROUP_HEADS * GLA_DK,
              GROUP_WIDTH,
              GROUP_WIDTH,
              LANE,
              2 * GROUP_WIDTH)
SEG_OFFS = tuple(int(v) for v in np.cumsum((0,) + SEG_WIDTHS))
D_IN_PAD = SEG_OFFS[-1]


def _cparams(sem):
    return pltpu.CompilerParams(dimension_semantics=sem, vmem_limit_bytes=VMEM_LIMIT)


def _bdot(a, b):
    return jnp.dot(a.astype(BF16), b.astype(BF16), preferred_element_type=F32)


def _dot_nt(a, b):
    return lax.dot_general(a.astype(BF16), b.astype(BF16), (((1,), (1,)), ((), ())),
                           preferred_element_type=F32)


def _split2(a):
    hi = a.astype(BF16)
    lo = (a - hi.astype(F32)).astype(BF16)
    return hi, lo


def _split3(a):
    hi = a.astype(BF16)
    r = a - hi.astype(F32)
    mid = r.astype(BF16)
    lo = (r - mid.astype(F32)).astype(BF16)
    return hi, mid, lo


def _dot_lhs3(a, b_exact):
    hi, mid, lo = _split3(a)
    b = b_exact.astype(BF16)
    d = lambda p: jnp.dot(p, b, preferred_element_type=F32)
    return d(hi) + d(mid) + d(lo)


def _dot_rhs3(a_exact, b):
    hi, mid, lo = _split3(b)
    a = a_exact.astype(BF16)
    d = lambda p: jnp.dot(a, p, preferred_element_type=F32)
    return d(hi) + d(mid) + d(lo)


def _dot_f32(a, b):
    ah, al = _split2(a)
    bh, bl = _split2(b)
    d = lambda p, q: jnp.dot(p, q, preferred_element_type=F32)
    return d(ah, bh) + d(al, bh) + d(ah, bl)


def _rms(x):
    return x * lax.rsqrt(jnp.mean(x * x, axis=-1, keepdims=True) + RMS_EPS)


def _sigmoid(x):
    return 1.0 / (1.0 + jnp.exp(-x))


def _group_mean_matrix(width, group):
    r = lax.broadcasted_iota(jnp.int32, (width, width), 0) // group
    c = lax.broadcasted_iota(jnp.int32, (width, width), 1) // group
    return jnp.where(r == c, 1.0 / group, 0.0).astype(BF16)


def _mod_kernel(c_ref, w_ref, b_ref, o_ref):
    c = c_ref[...]
    cond = c * _sigmoid(c)
    o_ref[0] = _dot_f32(cond, w_ref[0]) + b_ref[0]


def _modulation(c, w_mod, b_mod):
    depth, d, n = w_mod.shape
    b = c.shape[0]
    tn = D_MODEL
    return pl.pallas_call(
        _mod_kernel,
        out_shape=jax.ShapeDtypeStruct((depth, b, n), F32),
        grid=(depth, n // tn),
        in_specs=[pl.BlockSpec((b, d), lambda l, j: (0, 0)),
                  pl.BlockSpec((1, d, tn), lambda l, j: (l, 0, j)),
                  pl.BlockSpec((1, 1, tn), lambda l, j: (l, 0, j))],
        out_specs=pl.BlockSpec((1, b, tn), lambda l, j: (l, 0, j)),
        compiler_params=_cparams(("parallel", "parallel")),
        name="adaln_mod",
    )(c, w_mod, b_mod.reshape(depth, 1, n))


def _rope_table_kernel(pos_ref, freq_ref, sign_ref, cos_ref, sin_ref):
    ang = pos_ref[...].astype(F32) * freq_ref[...]
    cos_ref[...] = jnp.cos(ang)
    sin_ref[...] = jnp.sin(ang) * sign_ref[...]


def _rope_tables(positions):
    b, s = positions.shape
    rows = b * s
    tr = min(1024, rows)
    inv_freq = ROPE_THETA ** (-jnp.arange(0, ROPE_DIM, 2, dtype=F32) / ROPE_DIM)
    lane = np.arange(GROUP_WIDTH) % HEAD_DIM
    freq = jnp.where(lane < ROPE_DIM, inv_freq[lane % ROPE_HALF], 0.0).astype(F32)
    sign = jnp.asarray(np.where(lane < ROPE_HALF, -1.0, 1.0), dtype=F32)
    shape = jax.ShapeDtypeStruct((rows, GROUP_WIDTH), F32)
    return pl.pallas_call(
        _rope_table_kernel,
        out_shape=(shape, shape),
        grid=(rows // tr,),
        in_specs=[pl.BlockSpec((tr, 1), lambda i: (i, 0)),
                  pl.BlockSpec((1, GROUP_WIDTH), lambda i: (0, 0)),
                  pl.BlockSpec((1, GROUP_WIDTH), lambda i: (0, 0))],
        out_specs=(pl.BlockSpec((tr, GROUP_WIDTH), lambda i: (i, 0)),
                   pl.BlockSpec((tr, GROUP_WIDTH), lambda i: (i, 0))),
        compiler_params=_cparams(("parallel",)),
        name="rope_tables",
    )(positions.reshape(rows, 1), freq.reshape(1, -1), sign.reshape(1, -1))


def _inproj_kernel(x_ref, mod_ref, g_ref, w_ref, *out_refs):
    h = _rms(x_ref[...]) * g_ref[0:1, :]
    h = h * (1.0 + mod_ref[0, 1:2, :]) + mod_ref[0, 0:1, :]
    hb = h.astype(BF16)
    for k, o_ref in enumerate(out_refs):
        o_ref[...] = jnp.dot(hb, w_ref[:, SEG_OFFS[k]:SEG_OFFS[k + 1]],
                             preferred_element_type=F32)


def _in_projection(x2, mod_l, norm_g_l, w_pad, seq, tm):
    rows = x2.shape[0]
    tpb = seq // tm
    outs = tuple(jax.ShapeDtypeStruct((rows, w), F32) for w in SEG_WIDTHS)
    return pl.pallas_call(
        _inproj_kernel,
        out_shape=outs,
        grid=(rows // tm,),
        in_specs=[pl.BlockSpec((tm, D_MODEL), lambda i: (i, 0)),
                  pl.BlockSpec((1, N_MOD, D_MODEL), lambda i: (i // tpb, 0, 0)),
                  pl.BlockSpec((4, D_MODEL), lambda i: (0, 0)),
                  pl.BlockSpec((D_MODEL, D_IN_PAD), lambda i: (0, 0))],
        out_specs=tuple(pl.BlockSpec((tm, w), lambda i: (i, 0)) for w in SEG_WIDTHS),
        compiler_params=_cparams(("parallel",)),
        name="in_proj",
    )(x2, mod_l, norm_g_l, w_pad)


def _rope(x, cos, sin):
    w = x.shape[-1]
    lane = lax.broadcasted_iota(jnp.int32, x.shape, 1) % HEAD_DIM
    swapped = jnp.where(lane < ROPE_HALF,
                        pltpu.roll(x, w - ROPE_HALF, axis=1),
                        pltpu.roll(x, ROPE_HALF, axis=1))
    return x * cos + swapped * sin


def _nsa_prep_kernel(q_ref, kv_ref, cos_ref, sin_ref,
                     qp_ref, qh_ref, ks_ref, vs_ref, kw_ref, vw_ref):
    cos = cos_ref[...]
    sin = sin_ref[...]
    q = q_ref[...]
    q_r = _rope(q, cos, sin)
    for h in range(N_GROUP_HEADS):
        qp_ref[0, h] = q[:, h * HEAD_DIM:(h + 1) * HEAD_DIM].astype(BF16)
        qh_ref[0, h] = q_r[:, h * HEAD_DIM:(h + 1) * HEAD_DIM].astype(BF16)
    lane = lax.broadcasted_iota(jnp.int32, (cos.shape[0], LANE), 1)
    is_k = lane < HEAD_DIM
    cos_kv = jnp.where(is_k, cos[:, :LANE], 1.0)
    sin_kv = jnp.where(is_k, sin[:, :LANE], 0.0)
    slc = _rope(kv_ref[:, LANE:2 * LANE], cos_kv, sin_kv)
    win = _rope(kv_ref[:, 2 * LANE:3 * LANE], cos_kv, sin_kv)
    ks_ref[0] = slc[:, :HEAD_DIM].astype(BF16)
    vs_ref[0] = slc[:, HEAD_DIM:].astype(BF16)
    kw_ref[0] = win[:, :HEAD_DIM].astype(BF16)
    vw_ref[0] = win[:, HEAD_DIM:].astype(BF16)


def _nsa_prep(q, kv, cos_t, sin_t, batch, seq, tp):
    tpb = seq // tp
    row = lambda b, i: (b * tpb + i, 0)
    kvs = jax.ShapeDtypeStruct((batch, seq, HEAD_DIM), BF16)
    kv_spec = pl.BlockSpec((1, tp, HEAD_DIM), lambda b, i: (b, i, 0))
    qs = jax.ShapeDtypeStruct((batch, N_GROUP_HEADS, seq, HEAD_DIM), BF16)
    q_spec = pl.BlockSpec((1, N_GROUP_HEADS, tp, HEAD_DIM), lambda b, i: (b, 0, i, 0))
    return pl.pallas_call(
        _nsa_prep_kernel,
        out_shape=(qs, qs, kvs, kvs, kvs, kvs),
        grid=(batch, tpb),
        in_specs=[pl.BlockSpec((tp, GROUP_WIDTH), row),
                  pl.BlockSpec((tp, 2 * N_BRANCH * HEAD_DIM), row),
                  pl.BlockSpec((tp, GROUP_WIDTH), row),
                  pl.BlockSpec((tp, GROUP_WIDTH), row)],
        out_specs=(q_spec, q_spec, kv_spec, kv_spec, kv_spec, kv_spec),
        compiler_params=_cparams(("parallel", "parallel")),
        name="nsa_prep",
    )(q, kv, cos_t, sin_t)


def _nsa_compress_kernel(kv_ref, pos_ref, w1_ref, w2_ref, kc_ref, vc_ref, *, n_half):
    half = CMP_BLOCK // 2
    assert half == CMP_STRIDE
    chunks = [kv_ref[pl.ds(r, n_half, stride=CMP_STRIDE), :] for r in range(half)]
    for j, o_ref in enumerate((kc_ref, vc_ref)):
        first = jnp.zeros((n_half, CMP_HIDDEN), F32)
        second = jnp.zeros((n_half, CMP_HIDDEN), F32)
        for r in range(half):
            t = chunks[r][:, j * HEAD_DIM:(j + 1) * HEAD_DIM]
            first += _bdot(t + pos_ref[j, r:r + 1, :],
                           w1_ref[j, r * HEAD_DIM:(r + 1) * HEAD_DIM, :])
            second += _bdot(t + pos_ref[j, half + r:half + r + 1, :],
                            w1_ref[j, (half + r) * HEAD_DIM:(half + r + 1) * HEAD_DIM, :])
        pre = first + pltpu.roll(second, n_half - 1, axis=0)
        row = lax.broadcasted_iota(jnp.int32, pre.shape, 0)
        pre = jnp.where(row < n_half - 1, pre, 0.0)
        hid = pre * _sigmoid(pre)
        o_ref[0] = _bdot(hid, w2_ref[j])


def _nsa_compress(kv, cmp_pos_l, cmp_w1_l, cmp_w2_l, batch, seq):
    n_half = seq // CMP_STRIDE
    out = jax.ShapeDtypeStruct((batch, n_half, HEAD_DIM), F32)
    return pl.pallas_call(
        functools.partial(_nsa_compress_kernel, n_half=n_half),
        out_shape=(out, out),
        grid=(batch,),
        in_specs=[pl.BlockSpec((seq, LANE), lambda b: (b, 0)),
                  pl.BlockSpec((2, CMP_BLOCK, HEAD_DIM), lambda b: (0, 0, 0)),
                  pl.BlockSpec((2, CMP_BLOCK * HEAD_DIM, CMP_HIDDEN), lambda b: (0, 0, 0)),
                  pl.BlockSpec((2, CMP_HIDDEN, HEAD_DIM), lambda b: (0, 0, 0))],
        out_specs=(pl.BlockSpec((1, n_half, HEAD_DIM), lambda b: (b, 0, 0)),
                   pl.BlockSpec((1, n_half, HEAD_DIM), lambda b: (b, 0, 0))),
        compiler_params=_cparams(("parallel",)),
        name="nsa_compress",
    )(kv, cmp_pos_l, cmp_w1_l, cmp_w2_l)


def _nsa_attn_kernel(qp_ref, qh_ref, kc_ref, vc_ref, ks_ref, vs_ref, kw_ref, vw_ref, gate_ref,
                     ovl_ref, exp_ref, o_ref, m_ref, l_ref, acc_ref,
                     *, tq, tk, seq, n_top, win_len):
    H = N_GROUP_HEADS
    scale = HEAD_DIM ** -0.5
    n_cmp_pad = kc_ref.shape[1]
    n_cmp = n_cmp_pad - 1
    n_blk = seq // SLC_BLOCK
    t0 = pl.program_id(1) * tq
    q = qh_ref[0].reshape(H * tq, HEAD_DIM)

    s_c = _dot_nt(qp_ref[0].reshape(H * tq, HEAD_DIM), kc_ref[0]).reshape(H, tq, n_cmp_pad) * scale
    t_c = t0 + lax.broadcasted_iota(jnp.int32, (H, tq, n_cmp_pad), 1)
    n_c = lax.broadcasted_iota(jnp.int32, (H, tq, n_cmp_pad), 2)
    valid_c = (n_c * CMP_STRIDE + CMP_BLOCK - 1 <= t_c) & (n_c < n_cmp)
    s_c = jnp.where(valid_c, s_c, NEG_INF)
    e_c = jnp.where(valid_c, jnp.exp(s_c - jnp.max(s_c, axis=-1, keepdims=True)), 0.0)
    den = jnp.sum(e_c, axis=-1, keepdims=True)
    p_c = e_c / jnp.where(den > 0.0, den, 1.0)
    o_cmp = _bdot(p_c.reshape(H * tq, n_cmp_pad), vc_ref[0])

    p_sum = p_c[0]
    for h in range(1, H):
        p_sum = p_sum + p_c[h]
    p_slc = _dot_lhs3(p_sum, ovl_ref[...])
    t_s = t0 + lax.broadcasted_iota(jnp.int32, (tq, n_blk), 0)
    j_s = lax.broadcasted_iota(jnp.int32, (tq, n_blk), 1)
    cur = t_s // SLC_BLOCK
    forced = (j_s < N_INIT_BLOCKS) | ((cur - j_s >= 0) & (cur - j_s < N_LOCAL_BLOCKS))
    score = jnp.where(j_s <= cur, p_slc + jnp.where(forced, FORCE_BONUS, 0.0), NEG_INF)
    rank = jnp.zeros((tq, n_blk), F32)
    for jp in range(n_blk):
        col = score[:, jp:jp + 1]
        rank = rank + jnp.where(j_s > jp, jnp.where(col >= score, 1.0, 0.0),
                                jnp.where(col > score, 1.0, 0.0))
    sel = jnp.where(rank < n_top, 1.0, 0.0).astype(BF16)

    m_ref[...] = jnp.full(m_ref.shape, NEG_INF, F32)
    l_ref[...] = jnp.zeros(l_ref.shape, F32)
    acc_ref[...] = jnp.zeros(acc_ref.shape, F32)
    t_k = t0 + lax.broadcasted_iota(jnp.int32, (H, tq, tk), 1)
    lane_k = lax.broadcasted_iota(jnp.int32, (H, tq, tk), 2)

    def key_tile(kt, carry):
        k0 = pl.multiple_of(kt * tk, tk)
        picked = jnp.dot(sel, exp_ref[:, pl.ds(k0, tk)], preferred_element_type=F32)
        valid = (jnp.broadcast_to(picked[None], (H, tq, tk)) > 0.5) & (k0 + lane_k <= t_k)
        s = _dot_nt(q, ks_ref[0, pl.ds(k0, tk), :]).reshape(H, tq, tk) * scale
        s = jnp.where(valid, s, NEG_INF)
        m_old = m_ref[...]
        m_new = jnp.maximum(m_old, jnp.max(s, axis=-1, keepdims=True))
        p = jnp.where(valid, jnp.exp(s - m_new), 0.0)
        alpha = jnp.exp(m_old - m_new)
        l_ref[...] = alpha * l_ref[...] + jnp.sum(p, axis=-1, keepdims=True)
        pv = _bdot(p.reshape(H * tq, tk), vs_ref[0, pl.ds(k0, tk), :])
        acc_ref[...] = alpha * acc_ref[...] + pv.reshape(H, tq, HEAD_DIM)
        m_ref[...] = m_new
        return carry

    lax.fori_loop(0, (t0 + tq + tk - 1) // tk, key_tile, 0)
    o_sel = acc_ref[...] / l_ref[...]

    w0 = pl.multiple_of(jnp.maximum(t0 + tq - win_len, 0), tq)
    s_w = _dot_nt(q, kw_ref[0, pl.ds(w0, win_len), :]).reshape(H, tq, win_len) * scale
    diff = (t0 + lax.broadcasted_iota(jnp.int32, (H, tq, win_len), 1)
            - w0 - lax.broadcasted_iota(jnp.int32, (H, tq, win_len), 2))
    valid_w = (diff >= 0) & (diff < WINDOW)
    s_w = jnp.where(valid_w, s_w, NEG_INF)
    e_w = jnp.where(valid_w, jnp.exp(s_w - jnp.max(s_w, axis=-1, keepdims=True)), 0.0)
    p_w = e_w / jnp.sum(e_w, axis=-1, keepdims=True)
    o_win = _bdot(p_w.reshape(H * tq, win_len), vw_ref[0, pl.ds(w0, win_len), :])

    o_cmp = o_cmp.reshape(H, tq, HEAD_DIM)
    o_win = o_win.reshape(H, tq, HEAD_DIM)
    g = _sigmoid(gate_ref[...])
    outs = []
    for h in range(H):
        c = h * N_BRANCH
        outs.append(g[:, c:c + 1] * o_cmp[h] + g[:, c + 1:c + 2] * o_sel[h]
                    + g[:, c + 2:c + 3] * o_win[h])
    o_ref[...] = jnp.concatenate(outs, axis=-1)


def _cmp_to_slc(n_cmp_pad, n_blk):
    n_cmp = n_cmp_pad - 1
    cs = np.arange(n_cmp) * CMP_STRIDE
    ss = np.arange(n_blk) * SLC_BLOCK
    ov = np.minimum(cs[:, None] + CMP_BLOCK, ss[None, :] + SLC_BLOCK) - np.maximum(cs[:, None], ss[None, :])
    m = np.zeros((n_cmp_pad, n_blk), np.float32)
    m[:n_cmp] = np.clip(ov, 0, None) / CMP_STRIDE
    return jnp.asarray(m, dtype=BF16)


def _nsa_attention(qp, qh, kc, vc, ks, vs, kw, vw, gates, batch, seq, tq, tk):
    n_blk = seq // SLC_BLOCK
    n_cmp_pad = kc.shape[1]
    n_top = min(N_SLC, n_blk)
    win_len = min(WINDOW + tq, seq)
    tpb = seq // tq
    ovl = _cmp_to_slc(n_cmp_pad, n_blk)
    expand = jnp.asarray(np.arange(seq)[None, :] // SLC_BLOCK == np.arange(n_blk)[:, None], dtype=BF16)
    full = lambda shape: pl.BlockSpec((1,) + shape, lambda b, i: (b, 0, 0))
    H = N_GROUP_HEADS
    return pl.pallas_call(
        functools.partial(_nsa_attn_kernel, tq=tq, tk=tk, seq=seq, n_top=n_top, win_len=win_len),
        out_shape=jax.ShapeDtypeStruct((batch * seq, GROUP_WIDTH), F32),
        grid=(batch, tpb),
        in_specs=[pl.BlockSpec((1, H, tq, HEAD_DIM), lambda b, i: (b, 0, i, 0)),
                  pl.BlockSpec((1, H, tq, HEAD_DIM), lambda b, i: (b, 0, i, 0)),
                  full((n_cmp_pad, HEAD_DIM)), full((n_cmp_pad, HEAD_DIM)),
                  full((seq, HEAD_DIM)), full((seq, HEAD_DIM)),
                  full((seq, HEAD_DIM)), full((seq, HEAD_DIM)),
                  pl.BlockSpec((tq, LANE), lambda b, i: (b * tpb + i, 0)),
                  pl.BlockSpec((n_cmp_pad, n_blk), lambda b, i: (0, 0)),
                  pl.BlockSpec((n_blk, seq), lambda b, i: (0, 0))],
        out_specs=pl.BlockSpec((tq, GROUP_WIDTH), lambda b, i: (b * tpb + i, 0)),
        scratch_shapes=[pltpu.VMEM((H, tq, 1), F32), pltpu.VMEM((H, tq, 1), F32),
                        pltpu.VMEM((H, tq, HEAD_DIM), F32)],
        compiler_params=_cparams(("parallel", "parallel")),
        name="nsa_attn",
    )(qp, qh, kc, vc, ks, vs, kw, vw, gates, ovl, expand)


def _gla_kernel(qk_ref, v_ref, g_ref, lr_ref, w2_ref, gb_ref, o_ref, state_ref, *, n_chunks):
    H, C = N_GROUP_HEADS, GLA_CHUNK
    DKW = H * GLA_DK
    DVW = H * GLA_DV

    @pl.when(pl.program_id(1) == 0)
    def _():
        state_ref[...] = jnp.zeros(state_ref.shape, F32)

    tri = (lax.broadcasted_iota(jnp.int32, (C, C), 0)
           >= lax.broadcasted_iota(jnp.int32, (C, C), 1))
    tri_b = jnp.where(tri, 1.0, 0.0).astype(BF16)
    qhead = lax.broadcasted_iota(jnp.int32, (C, DKW), 1) // GLA_DK
    vhead = lax.broadcasted_iota(jnp.int32, (C, DVW), 1) // GLA_DV
    causal = jnp.concatenate([tri] * H, axis=0)
    state_mask = (lax.broadcasted_iota(jnp.int32, (DKW, DVW), 0) // GLA_DK
                  == lax.broadcasted_iota(jnp.int32, (DKW, DVW), 1) // GLA_DV)
    gmean = _group_mean_matrix(DVW, GLA_DV)

    def chunk(ci, carry):
        r0 = pl.multiple_of(ci * C, C)
        rows = pl.ds(r0, C)
        pre = _dot_f32(lr_ref[rows, :], w2_ref[...]) + gb_ref[...]
        log_a = (jnp.minimum(pre, 0.0) - jnp.log1p(jnp.exp(-jnp.abs(pre)))) / GLA_TAU
        b = _dot_rhs3(tri_b, log_a)
        b_mid = b[C // 2:C // 2 + 1, :]
        b_last = b[C - 1:C, :]
        q = qk_ref[rows, 0:DKW] * (GLA_DK ** -0.5)
        k = qk_ref[rows, DKW:2 * DKW]
        v = v_ref[rows, :]
        q_in = q * jnp.exp(b - b_mid)
        k_in = k * jnp.exp(b_mid - b)
        q_stack = jnp.concatenate([jnp.where(qhead == h, q_in, 0.0) for h in range(H)], axis=0)
        att = jnp.where(causal, _dot_nt(q_stack, k_in), 0.0)
        o_stack = _bdot(att, v)
        o = _bdot(q * jnp.exp(b), state_ref[...])
        for h in range(H):
            o = o + jnp.where(vhead == h, o_stack[h * C:(h + 1) * C], 0.0)
        k_out = k * jnp.exp(b_last - b)
        kv = _bdot(k_out.T, v)
        decay = jnp.exp(jnp.broadcast_to(b_last, (8, DKW))).T[:, 0:1]
        state_ref[...] = decay * state_ref[...] + jnp.where(state_mask, kv, 0.0)
        ms = _dot_lhs3(o * o, gmean)
        gate = g_ref[rows, :]
        o_ref[rows, :] = gate * _sigmoid(gate) * (o * lax.rsqrt(ms + RMS_EPS))
        return carry

    lax.fori_loop(0, n_chunks, chunk, 0)


def _gla(qk, v, g, lr, gate_w2_l, gate_b_l, batch, seq, tg):
    tpb = seq // tg
    row = lambda b, i: (b * tpb + i, 0)
    w2 = jnp.zeros((LANE, N_GROUP_HEADS * GLA_DK), F32).at[:GLA_RANK].set(gate_w2_l)
    return pl.pallas_call(
        functools.partial(_gla_kernel, n_chunks=tg // GLA_CHUNK),
        out_shape=jax.ShapeDtypeStruct((batch * seq, GROUP_WIDTH), F32),
        grid=(batch, tpb),
        in_specs=[pl.BlockSpec((tg, 2 * N_GROUP_HEADS * GLA_DK), row),
                  pl.BlockSpec((tg, GROUP_WIDTH), row),
                  pl.BlockSpec((tg, GROUP_WIDTH), row),
                  pl.BlockSpec((tg, LANE), row),
                  pl.BlockSpec((LANE, N_GROUP_HEADS * GLA_DK), lambda b, i: (0, 0)),
                  pl.BlockSpec((1, N_GROUP_HEADS * GLA_DK), lambda b, i: (0, 0))],
        out_specs=pl.BlockSpec((tg, GROUP_WIDTH), row),
        scratch_shapes=[pltpu.VMEM((N_GROUP_HEADS * GLA_DK, N_GROUP_HEADS * GLA_DV), F32)],
        compiler_params=_cparams(("parallel", "arbitrary")),
        name="gla",
    )(qk, v, g, lr, w2, gate_b_l.reshape(1, -1))


def _gmlp_kernel(uv_ref, w_ref, bias_ref, o_ref, *, n_chunks):
    T, G = GMLP_CHUNK, N_GROUP_HEADS
    gdim = GROUP_WIDTH // G
    gmean = _group_mean_matrix(GROUP_WIDTH, gdim)
    tri = (lax.broadcasted_iota(jnp.int32, (T, T), 0)
           >= lax.broadcasted_iota(jnp.int32, (T, T), 1))
    head = lax.broadcasted_iota(jnp.int32, (T, GROUP_WIDTH), 1) // gdim
    w = [jnp.where(tri, w_ref[g], 0.0).astype(BF16) for g in range(G)]
    for ci in range(n_chunks):
        rows = slice(ci * T, (ci + 1) * T)
        u = uv_ref[rows, 0:GROUP_WIDTH]
        v = uv_ref[rows, GROUP_WIDTH:2 * GROUP_WIDTH]
        d = v - _dot_lhs3(v, gmean)
        vn = (d * lax.rsqrt(_dot_lhs3(d * d, gmean) + RMS_EPS)).astype(BF16)
        mixed = bias_ref[...]
        for g in range(G):
            mixed = mixed + jnp.where(head == g, jnp.dot(w[g], vn, preferred_element_type=F32), 0.0)
        o_ref[rows, :] = u * mixed


def _gmlp(uv, gmlp_ws_l, gmlp_b_l, rows, tc):
    gdim = GROUP_WIDTH // N_GROUP_HEADS
    bias = jnp.repeat(gmlp_b_l.T, gdim, axis=1)
    return pl.pallas_call(
        functools.partial(_gmlp_kernel, n_chunks=tc // GMLP_CHUNK),
        out_shape=jax.ShapeDtypeStruct((rows, GROUP_WIDTH), F32),
        grid=(rows // tc,),
        in_specs=[pl.BlockSpec((tc, 2 * GROUP_WIDTH), lambda i: (i, 0)),
                  pl.BlockSpec((N_GROUP_HEADS, GMLP_CHUNK, GMLP_CHUNK), lambda i: (0, 0, 0)),
                  pl.BlockSpec((GMLP_CHUNK, GROUP_WIDTH), lambda i: (0, 0))],
        out_specs=pl.BlockSpec((tc, GROUP_WIDTH), lambda i: (i, 0)),
        compiler_params=_cparams(("parallel",)),
        name="gmlp",
    )(uv, gmlp_ws_l, bias)


def _outproj_kernel(conv_ref, halo_ref, cw_ref, nsa_ref, gla_ref, gmlp_ref, gain_ref, wo_ref,
                    x_ref, mod_ref, g_ref, o_ref, *, tpb):
    GW = GROUP_WIDTH
    tm = conv_ref.shape[0]
    z = conv_ref[:, 2 * GW:3 * GW] * conv_ref[:, 0:GW]
    zh = halo_ref[:, 2 * GW:3 * GW] * halo_ref[:, 0:GW]
    zh = jnp.where(pl.program_id(0) % tpb == 0, 0.0, zh)
    row = lax.broadcasted_iota(jnp.int32, (tm, GW), 0)
    y = cw_ref[CONV_WIDTH - 1:CONV_WIDTH, :] * z
    for back in range(1, CONV_WIDTH):
        shifted = pltpu.roll(z, back, axis=0)
        for r in range(back):
            shifted = jnp.where(row == r, zh[8 - back + r:8 - back + r + 1, :], shifted)
        y = y + cw_ref[CONV_WIDTH - 1 - back:CONV_WIDTH - back, :] * shifted
    groups = (conv_ref[:, GW:2 * GW] * y, nsa_ref[...], gla_ref[...], gmlp_ref[...])
    acc = jnp.zeros((tm, D_MODEL), F32)
    for k, o in enumerate(groups):
        n = (_rms(o) * gain_ref[k:k + 1, :]).astype(BF16)
        acc = acc + jnp.dot(n, wo_ref[k * GW:(k + 1) * GW, :], preferred_element_type=F32)
    o_ref[...] = x_ref[...] + mod_ref[0, 2:3, :] * (_rms(acc) * g_ref[1:2, :])


def _out_projection(conv, nsa, gla, gmlp, conv_w_l, grp_gain_l, wo_b, x2, mod_l, norm_g_l, seq, tm):
    rows = x2.shape[0]
    tpb = seq // tm
    hpt = tm // 8
    gw = lambda i: (i, 0)
    const = lambda i: (0, 0)
    return pl.pallas_call(
        functools.partial(_outproj_kernel, tpb=tpb),
        out_shape=jax.ShapeDtypeStruct((rows, D_MODEL), F32),
        grid=(rows // tm,),
        in_specs=[pl.BlockSpec((tm, 3 * GROUP_WIDTH), gw),
                  pl.BlockSpec((8, 3 * GROUP_WIDTH), lambda i: (jnp.maximum(i * hpt - 1, 0), 0)),
                  pl.BlockSpec((CONV_WIDTH, GROUP_WIDTH), const),
                  pl.BlockSpec((tm, GROUP_WIDTH), gw),
                  pl.BlockSpec((tm, GROUP_WIDTH), gw),
                  pl.BlockSpec((tm, GROUP_WIDTH), gw),
                  pl.BlockSpec((N_MIXERS, GROUP_WIDTH), const),
                  pl.BlockSpec((D_MODEL, D_MODEL), const),
                  pl.BlockSpec((tm, D_MODEL), gw),
                  pl.BlockSpec((1, N_MOD, D_MODEL), lambda i: (i // tpb, 0, 0)),
                  pl.BlockSpec((4, D_MODEL), const)],
        out_specs=pl.BlockSpec((tm, D_MODEL), gw),
        compiler_params=_cparams(("parallel",)),
        name="out_proj",
    )(conv, conv, conv_w_l, nsa, gla, gmlp, grp_gain_l.reshape(N_MIXERS, GROUP_WIDTH), wo_b,
      x2, mod_l, norm_g_l)


def _mlp_kernel(x_ref, mod_ref, g_ref, wu_ref, wd_ref, o_ref, *, tf):
    x = x_ref[...]
    h = _rms(x) * g_ref[2:3, :]
    hb = (h * (1.0 + mod_ref[0, 4:5, :]) + mod_ref[0, 3:4, :]).astype(BF16)
    acc = jnp.zeros(x.shape, F32)
    for f in range(D_FF // tf):
        u = jnp.dot(hb, wu_ref[:, f * tf:(f + 1) * tf], preferred_element_type=F32)
        u = jnp.maximum(u, 0.0)
        acc = acc + jnp.dot((u * u).astype(BF16), wd_ref[f * tf:(f + 1) * tf, :],
                            preferred_element_type=F32)
    o_ref[...] = x + mod_ref[0, 5:6, :] * (_rms(acc) * g_ref[3:4, :])


def _mlp(x2, mod_l, norm_g_l, wu_b, wd_b, seq, tm):
    rows = x2.shape[0]
    tpb = seq // tm
    const = lambda i: (0, 0)
    return pl.pallas_call(
        functools.partial(_mlp_kernel, tf=D_MODEL),
        out_shape=jax.ShapeDtypeStruct((rows, D_MODEL), F32),
        grid=(rows // tm,),
        in_specs=[pl.BlockSpec((tm, D_MODEL), lambda i: (i, 0)),
                  pl.BlockSpec((1, N_MOD, D_MODEL), lambda i: (i // tpb, 0, 0)),
                  pl.BlockSpec((4, D_MODEL), const),
                  pl.BlockSpec((D_MODEL, D_FF), const, pipeline_mode=pl.Buffered(1)),
                  pl.BlockSpec((D_FF, D_MODEL), const, pipeline_mode=pl.Buffered(1))],
        out_specs=pl.BlockSpec((tm, D_MODEL), lambda i: (i, 0)),
        compiler_params=_cparams(("parallel",)),
        name="mlp",
    )(x2, mod_l, norm_g_l, wu_b, wd_b)


def _pad_in_proj(w_in_l):
    offs = np.cumsum((0,) + (3 * GROUP_WIDTH, GROUP_WIDTH, 2 * N_BRANCH * HEAD_DIM, N_BRANCH * N_GROUP_HEADS,
                             2 * N_GROUP_HEADS * GLA_DK, GROUP_WIDTH, GROUP_WIDTH, GLA_RANK,
                             2 * GROUP_WIDTH))
    parts = []
    for k, w in enumerate(SEG_WIDTHS):
        seg = w_in_l[:, offs[k]:offs[k + 1]]
        parts.append(jnp.pad(seg, ((0, 0), (0, w - seg.shape[1]))))
    return jnp.concatenate(parts, axis=1).astype(BF16)


def kernel(x, c, positions, w_in, conv_w, cmp_pos, cmp_w1, cmp_w2, gla_gate_w2, gla_gate_b, gmlp_ws, gmlp_b, grp_gain, w_o, norm_g, w_mod, b_mod, w_up, w_down):
    batch, seq, _ = x.shape
    depth = w_in.shape[0]
    rows = batch * seq
    tm = min(512, seq)
    tq = min(128, seq)
    tk = min(512, seq)

    mod = _modulation(c, w_mod, b_mod).reshape(depth, batch, N_MOD, D_MODEL)
    cos_t, sin_t = _rope_tables(positions)
    x2 = x.reshape(rows, D_MODEL)
    for l in range(depth):
        conv, n_q, n_kv, n_g, l_qk, l_v, l_g, l_lr, m_uv = _in_projection(
            x2, mod[l], norm_g[l], _pad_in_proj(w_in[l]), seq, tm)
        qp, qh, ks, vs, kw, vw = _nsa_prep(n_q, n_kv, cos_t, sin_t, batch, seq, tm)
        kc, vc = _nsa_compress(n_kv, cmp_pos[l], cmp_w1[l], cmp_w2[l], batch, seq)
        nsa = _nsa_attention(qp, qh, kc, vc, ks, vs, kw, vw, n_g, batch, seq, tq, tk)
        gla = _gla(l_qk, l_v, l_g, l_lr, gla_gate_w2[l], gla_gate_b[l], batch, seq, tm)
        gmlp = _gmlp(m_uv, gmlp_ws[l], gmlp_b[l], rows, tm)
        x2 = _out_projection(conv, nsa, gla, gmlp, conv_w[l], grp_gain[l], w_o[l].astype(BF16),
                             x2, mod[l], norm_g[l], seq, tm)
        x2 = _mlp(x2, mod[l], norm_g[l], w_up[l].astype(BF16), w_down[l].astype(BF16), seq, tm)
    return x2.reshape(batch, seq, D_MODEL)
```

```python
import functools

import numpy as np
import jax
import jax.numpy as jnp
from jax import lax
from jax.experimental import pallas as pl
from jax.experimental.pallas import tpu as pltpu

D_MODEL = 1024
N_MIXERS = 4
GROUP_WIDTH = D_MODEL // N_MIXERS
HEAD_DIM = 64
N_GROUP_HEADS = GROUP_WIDTH // HEAD_DIM
D_FF = 4 * D_MODEL
N_MOD = 6
RMS_EPS = 1e-6
NEG_INF = -1e30

ROPE_THETA = 500000.0
ROPE_DIM = HEAD_DIM // 4
ROPE_HALF = ROPE_DIM // 2

CONV_WIDTH = 3

CMP_BLOCK = 32
CMP_STRIDE = 16
CMP_HIDDEN = 2 * HEAD_DIM
SLC_BLOCK = 64
N_SLC = 16
N_INIT_BLOCKS = 1
N_LOCAL_BLOCKS = 2
FORCE_BONUS = 1e4
WINDOW = 512
N_BRANCH = 3

GLA_DK = HEAD_DIM // 2
GLA_DV = HEAD_DIM
GLA_RANK = 16
GLA_TAU = 16.0
GLA_CHUNK = 64

GMLP_CHUNK = 128

LANE = 128
VMEM_LIMIT = 48 * 1024 * 1024

F32 = jnp.float32
BF16 = jnp.bfloat16

SEG_WIDTHS = (3 * GROUP_WIDTH,
              GROUP_WIDTH,
              2 * N_BRANCH * HEAD_DIM,
              LANE,
              2 * N_GROUP_HEADS * GLA_DK,
              GROUP_WIDTH,
              GROUP_WIDTH,
              LANE,
              2 * GROUP_WIDTH)
SEG_OFFS = tuple(int(v) for v in np.cumsum((0,) + SEG_WIDTHS))
D_IN_PAD = SEG_OFFS[-1]


def _cparams(sem):
    return pltpu.CompilerParams(dimension_semantics=sem, vmem_limit_bytes=VMEM_LIMIT)


def _bdot(a, b):
    return jnp.dot(a.astype(BF16), b.astype(BF16), preferred_element_type=F32)


def _dot_nt(a, b):
    return lax.dot_general(a.astype(BF16), b.astype(BF16), (((1,), (1,)), ((), ())),
                           preferred_element_type=F32)


def _split2(a):
    hi = a.astype(BF16)
    lo = (a - hi.astype(F32)).astype(BF16)
    return hi, lo


def _split3(a):
    hi = a.astype(BF16)
    r = a - hi.astype(F32)
    mid = r.astype(BF16)
    lo = (r - mid.astype(F32)).astype(BF16)
    return hi, mid, lo


def _dot_lhs3(a, b_exact):
    hi, mid, lo = _split3(a)
    b = b_exact.astype(BF16)
    d = lambda p: jnp.dot(p, b, preferred_element_type=F32)
    return d(hi) + d(mid) + d(lo)


def _dot_rhs3(a_exact, b):
    hi, mid, lo = _split3(b)
    a = a_exact.astype(BF16)
    d = lambda p: jnp.dot(a, p, preferred_element_type=F32)
    return d(hi) + d(mid) + d(lo)


def _dot_f32(a, b):
    ah, al = _split2(a)
    bh, bl = _split2(b)
    d = lambda p, q: jnp.dot(p, q, preferred_element_type=F32)
    return d(ah, bh) + d(al, bh) + d(ah, bl)


def _rms(x):
    return x * lax.rsqrt(jnp.mean(x * x, axis=-1, keepdims=True) + RMS_EPS)


def _sigmoid(x):
    return 1.0 / (1.0 + jnp.exp(-x))


def _group_mean_matrix(width, group):
    r = lax.broadcasted_iota(jnp.int32, (width, width), 0) // group
    c = lax.broadcasted_iota(jnp.int32, (width, width), 1) // group
    return jnp.where(r == c, 1.0 / group, 0.0).astype(BF16)


def _mod_kernel(c_ref, w_ref, b_ref, o_ref):
    c = c_ref[...]
    cond = c * _sigmoid(c)
    o_ref[0] = _dot_f32(cond, w_ref[0]) + b_ref[0]


def _modulation(c, w_mod, b_mod):
    depth, d, n = w_mod.shape
    b = c.shape[0]
    tn = D_MODEL
    return pl.pallas_call(
        _mod_kernel,
        out_shape=jax.ShapeDtypeStruct((depth, b, n), F32),
        grid=(depth, n // tn),
        in_specs=[pl.BlockSpec((b, d), lambda l, j: (0, 0)),
                  pl.BlockSpec((1, d, tn), lambda l, j: (l, 0, j)),
                  pl.BlockSpec((1, 1, tn), lambda l, j: (l, 0, j))],
        out_specs=pl.BlockSpec((1, b, tn), lambda l, j: (l, 0, j)),
        compiler_params=_cparams(("parallel", "parallel")),
        name="adaln_mod",
    )(c, w_mod, b_mod.reshape(depth, 1, n))


def _rope_table_kernel(pos_ref, freq_ref, sign_ref, cos_ref, sin_ref):
    ang = pos_ref[...].astype(F32) * freq_ref[...]
    cos_ref[...] = jnp.cos(ang)
    sin_ref[...] = jnp.sin(ang) * sign_ref[...]


def _rope_tables(positions):
    b, s = positions.shape
    rows = b * s
    tr = min(1024, rows)
    inv_freq = ROPE_THETA ** (-jnp.arange(0, ROPE_DIM, 2, dtype=F32) / ROPE_DIM)
    lane = np.arange(GROUP_WIDTH) % HEAD_DIM
    freq = jnp.where(lane < ROPE_DIM, inv_freq[lane % ROPE_HALF], 0.0).astype(F32)
    sign = jnp.asarray(np.where(lane < ROPE_HALF, -1.0, 1.0), dtype=F32)
    shape = jax.ShapeDtypeStruct((rows, GROUP_WIDTH), F32)
    return pl.pallas_call(
        _rope_table_kernel,
        out_shape=(shape, shape),
        grid=(rows // tr,),
        in_specs=[pl.BlockSpec((tr, 1), lambda i: (i, 0)),
                  pl.BlockSpec((1, GROUP_WIDTH), lambda i: (0, 0)),
                  pl.BlockSpec((1, GROUP_WIDTH), lambda i: (0, 0))],
        out_specs=(pl.BlockSpec((tr, GROUP_WIDTH), lambda i: (i, 0)),
                   pl.BlockSpec((tr, GROUP_WIDTH), lambda i: (i, 0))),
        compiler_params=_cparams(("parallel",)),
        name="rope_tables",
    )(positions.reshape(rows, 1), freq.reshape(1, -1), sign.reshape(1, -1))


def _inproj_kernel(x_ref, mod_ref, g_ref, w_ref, *out_refs):
    h = _rms(x_ref[...]) * g_ref[0:1, :]
    h = h * (1.0 + mod_ref[0, 1:2, :]) + mod_ref[0, 0:1, :]
    hb = h.astype(BF16)
    for k, o_ref in enumerate(out_refs):
        o_ref[...] = jnp.dot(hb, w_ref[:, SEG_OFFS[k]:SEG_OFFS[k + 1]],
                             preferred_element_type=F32)


def _in_projection(x2, mod_l, norm_g_l, w_pad, seq, tm):
    rows = x2.shape[0]
    tpb = seq // tm
    outs = tuple(jax.ShapeDtypeStruct((rows, w), F32) for w in SEG_WIDTHS)
    return pl.pallas_call(
        _inproj_kernel,
        out_shape=outs,
        grid=(rows // tm,),
        in_specs=[pl.BlockSpec((tm, D_MODEL), lambda i: (i, 0)),
                  pl.BlockSpec((1, N_MOD, D_MODEL), lambda i: (i // tpb, 0, 0)),
                  pl.BlockSpec((4, D_MODEL), lambda i: (0, 0)),
                  pl.BlockSpec((D_MODEL, D_IN_PAD), lambda i: (0, 0))],
        out_specs=tuple(pl.BlockSpec((tm, w), lambda i: (i, 0)) for w in SEG_WIDTHS),
        compiler_params=_cparams(("parallel",)),
        name="in_proj",
    )(x2, mod_l, norm_g_l, w_pad)


def _rope(x, cos, sin):
    w = x.shape[-1]
    lane = lax.broadcasted_iota(jnp.int32, x.shape, 1) % HEAD_DIM
    swapped = jnp.where(lane < ROPE_HALF,
                        pltpu.roll(x, w - ROPE_HALF, axis=1),
                        pltpu.roll(x, ROPE_HALF, axis=1))
    return x * cos + swapped * sin


def _nsa_prep_kernel(q_ref, kv_ref, cos_ref, sin_ref,
                     qp_ref, qr_ref, ks_ref, vs_ref, kw_ref, vw_ref):
    cos = cos_ref[...]
    sin = sin_ref[...]
    q = q_ref[...] * (HEAD_DIM ** -0.5)
    qp_ref[0] = q.T.astype(BF16)
    qr_ref[0] = _rope(q, cos, sin).T.astype(BF16)
    lane = lax.broadcasted_iota(jnp.int32, (cos.shape[0], LANE), 1)
    is_k = lane < HEAD_DIM
    cos_kv = jnp.where(is_k, cos[:, :LANE], 1.0)
    sin_kv = jnp.where(is_k, sin[:, :LANE], 0.0)
    slc = _rope(kv_ref[:, LANE:2 * LANE], cos_kv, sin_kv)
    win = _rope(kv_ref[:, 2 * LANE:3 * LANE], cos_kv, sin_kv)
    ks_ref[0] = slc[:, :HEAD_DIM].astype(BF16)
    vs_ref[0] = slc.T[HEAD_DIM:, :].astype(BF16)
    kw_ref[0] = win[:, :HEAD_DIM].astype(BF16)
    vw_ref[0] = win.T[HEAD_DIM:, :].astype(BF16)


def _nsa_prep(q, kv, cos_t, sin_t, batch, seq, tp):
    tpb = seq // tp
    row = lambda b, i: (b * tpb + i, 0)
    qs = jax.ShapeDtypeStruct((batch, GROUP_WIDTH, seq), BF16)
    ks = jax.ShapeDtypeStruct((batch, seq, HEAD_DIM), BF16)
    vs = jax.ShapeDtypeStruct((batch, HEAD_DIM, seq), BF16)
    q_spec = pl.BlockSpec((1, GROUP_WIDTH, tp), lambda b, i: (b, 0, i))
    k_spec = pl.BlockSpec((1, tp, HEAD_DIM), lambda b, i: (b, i, 0))
    v_spec = pl.BlockSpec((1, HEAD_DIM, tp), lambda b, i: (b, 0, i))
    return pl.pallas_call(
        _nsa_prep_kernel,
        out_shape=(qs, qs, ks, vs, ks, vs),
        grid=(batch, tpb),
        in_specs=[pl.BlockSpec((tp, GROUP_WIDTH), row),
                  pl.BlockSpec((tp, 2 * N_BRANCH * HEAD_DIM), row),
                  pl.BlockSpec((tp, GROUP_WIDTH), row),
                  pl.BlockSpec((tp, GROUP_WIDTH), row)],
        out_specs=(q_spec, q_spec, k_spec, v_spec, k_spec, v_spec),
        compiler_params=_cparams(("parallel", "parallel")),
        name="nsa_prep",
    )(q, kv, cos_t, sin_t)


def _nsa_compress_kernel(kv_ref, pos_ref, w1_ref, w2_ref, kc_ref, vc_ref, *, n_half):
    half = CMP_BLOCK // 2
    assert half == CMP_STRIDE
    chunks = [kv_ref[pl.ds(r, n_half, stride=CMP_STRIDE), :] for r in range(half)]
    outs = []
    for j in range(2):
        first = jnp.zeros((n_half, CMP_HIDDEN), F32)
        second = jnp.zeros((n_half, CMP_HIDDEN), F32)
        for r in range(half):
            t = chunks[r][:, j * HEAD_DIM:(j + 1) * HEAD_DIM]
            first += _bdot(t + pos_ref[j, r:r + 1, :],
                           w1_ref[j, r * HEAD_DIM:(r + 1) * HEAD_DIM, :])
            second += _bdot(t + pos_ref[j, half + r:half + r + 1, :],
                            w1_ref[j, (half + r) * HEAD_DIM:(half + r + 1) * HEAD_DIM, :])
        pre = first + pltpu.roll(second, n_half - 1, axis=0)
        row = lax.broadcasted_iota(jnp.int32, pre.shape, 0)
        pre = jnp.where(row < n_half - 1, pre, 0.0)
        hid = pre * _sigmoid(pre)
        outs.append(_bdot(hid, w2_ref[j]))
    kc_ref[0] = outs[0].astype(BF16)
    vc_ref[0] = jnp.concatenate(outs, axis=1).T[HEAD_DIM:, :].astype(BF16)


def _nsa_compress(kv, cmp_pos_l, cmp_w1_l, cmp_w2_l, batch, seq):
    n_half = seq // CMP_STRIDE
    return pl.pallas_call(
        functools.partial(_nsa_compress_kernel, n_half=n_half),
        out_shape=(jax.ShapeDtypeStruct((batch, n_half, HEAD_DIM), BF16),
                   jax.ShapeDtypeStruct((batch, HEAD_DIM, n_half), BF16)),
        grid=(batch,),
        in_specs=[pl.BlockSpec((seq, LANE), lambda b: (b, 0)),
                  pl.BlockSpec((2, CMP_BLOCK, HEAD_DIM), lambda b: (0, 0, 0)),
                  pl.BlockSpec((2, CMP_BLOCK * HEAD_DIM, CMP_HIDDEN), lambda b: (0, 0, 0)),
                  pl.BlockSpec((2, CMP_HIDDEN, HEAD_DIM), lambda b: (0, 0, 0))],
        out_specs=(pl.BlockSpec((1, n_half, HEAD_DIM), lambda b: (b, 0, 0)),
                   pl.BlockSpec((1, HEAD_DIM, n_half), lambda b: (b, 0, 0))),
        compiler_params=_cparams(("parallel",)),
        name="nsa_compress",
    )(kv, cmp_pos_l, cmp_w1_l, cmp_w2_l)


def _nsa_attn_kernel(qp_ref, qr_ref, kc_ref, vc_ref, ks_ref, vs_ref, kw_ref, vw_ref, gate_ref,
                     ovl_ref, o_ref, sel_ref, m_ref, l_ref, acc_ref,
                     *, tq, qs, tk, seq, n_top, win_len):
    H = N_GROUP_HEADS
    W = H * qs
    G = tq // qs
    n_cmp_pad = kc_ref.shape[1]
    n_cmp = n_cmp_pad - 1
    n_blk = seq // SLC_BLOCK
    blk_per_tile = tk // SLC_BLOCK
    t_step = pl.program_id(1) * tq

    def heads_on_lanes(ref, g):
        return jnp.concatenate([ref[0, h * HEAD_DIM:(h + 1) * HEAD_DIM, g * qs:(g + 1) * qs]
                                for h in range(H)], axis=1)

    def tile_heads(a):
        return jnp.concatenate([a] * H, axis=1)

    q_rot, o_cmp, o_win = [], [], []
    for g in range(G):
        t0 = t_step + g * qs
        q_plain = heads_on_lanes(qp_ref, g)
        q_rot.append(heads_on_lanes(qr_ref, g))

        n_c = lax.broadcasted_iota(jnp.int32, (n_cmp_pad, qs), 0)
        t_c = t0 + lax.broadcasted_iota(jnp.int32, (n_cmp_pad, qs), 1)
        valid_c = tile_heads(jnp.where((n_c * CMP_STRIDE + CMP_BLOCK - 1 <= t_c) & (n_c < n_cmp),
                                       1.0, 0.0)) > 0.5
        s_c = jnp.where(valid_c, jnp.dot(kc_ref[0], q_plain, preferred_element_type=F32), NEG_INF)
        e_c = jnp.where(valid_c, jnp.exp(s_c - jnp.max(s_c, axis=0, keepdims=True)), 0.0)
        den = jnp.sum(e_c, axis=0, keepdims=True)
        p_c = e_c / jnp.where(den > 0.0, den, 1.0)
        o_cmp.append(jnp.dot(vc_ref[0], p_c.astype(BF16), preferred_element_type=F32))

        p_sum = p_c[:, 0:qs]
        for h in range(1, H):
            p_sum = p_sum + p_c[:, h * qs:(h + 1) * qs]
        p_slc = _dot_rhs3(ovl_ref[...], p_sum)
        j_s = lax.broadcasted_iota(jnp.int32, (n_blk, qs), 0)
        cur = (t0 + lax.broadcasted_iota(jnp.int32, (n_blk, qs), 1)) // SLC_BLOCK
        forced = (j_s < N_INIT_BLOCKS) | ((cur - j_s >= 0) & (cur - j_s < N_LOCAL_BLOCKS))
        score = jnp.where(j_s <= cur, p_slc + jnp.where(forced, FORCE_BONUS, 0.0), NEG_INF)
        rank = jnp.zeros((n_blk, qs), F32)
        for jp in range(n_blk):
            row = score[jp:jp + 1, :]
            rank = rank + jnp.where(j_s > jp, jnp.where(row >= score, 1.0, 0.0),
                                    jnp.where(row > score, 1.0, 0.0))
        sel_ref[g] = jnp.where(rank < n_top, 1.0, 0.0)

        w0 = pl.multiple_of(jnp.maximum(t0 + qs - win_len, 0), qs)
        diff = (t0 + lax.broadcasted_iota(jnp.int32, (win_len, qs), 1)
                - w0 - lax.broadcasted_iota(jnp.int32, (win_len, qs), 0))
        bias_w = jnp.where((diff >= 0) & (diff < WINDOW), 0.0, NEG_INF)
        s_w = (jnp.dot(kw_ref[0, pl.ds(w0, win_len), :], q_rot[g], preferred_element_type=F32)
               + tile_heads(bias_w))
        e_w = jnp.exp(s_w - jnp.max(s_w, axis=0, keepdims=True))
        o_w = jnp.dot(vw_ref[0, :, pl.ds(w0, win_len)], e_w.astype(BF16), preferred_element_type=F32)
        o_win.append(o_w / jnp.sum(e_w, axis=0, keepdims=True))

    m_ref[...] = jnp.full(m_ref.shape, NEG_INF, F32)
    l_ref[...] = jnp.zeros(l_ref.shape, F32)
    acc_ref[...] = jnp.zeros(acc_ref.shape, F32)
    k_row = lax.broadcasted_iota(jnp.int32, (tk, qs), 0)
    t_col = lax.broadcasted_iota(jnp.int32, (tk, qs), 1)

    def key_tile(kt, carry):
        k0 = pl.multiple_of(kt * tk, tk)
        keys = ks_ref[0, pl.ds(k0, tk), :]
        vals = vs_ref[0, :, pl.ds(k0, tk)]
        for g in range(G):
            picked = jnp.concatenate(
                [jnp.broadcast_to(sel_ref[g, pl.ds(kt * blk_per_tile + jj, 1), :], (SLC_BLOCK, qs))
                 for jj in range(blk_per_tile)], axis=0)
            causal = k0 + k_row <= t_step + g * qs + t_col
            bias = jnp.where((picked > 0.5) & causal, 0.0, NEG_INF)
            s = jnp.dot(keys, q_rot[g], preferred_element_type=F32) + tile_heads(bias)
            m_old = m_ref[g]
            m_new = jnp.maximum(m_old, jnp.max(s, axis=0, keepdims=True))
            p = jnp.exp(s - m_new)
            alpha = jnp.exp(m_old - m_new)
            l_ref[g] = alpha * l_ref[g] + jnp.sum(p, axis=0, keepdims=True)
            acc_ref[g] = alpha * acc_ref[g] + jnp.dot(vals, p.astype(BF16), preferred_element_type=F32)
            m_ref[g] = m_new
        return carry

    lax.fori_loop(0, (t_step + tq + tk - 1) // tk, key_tile, 0)

    for g in range(G):
        o_sel = acc_ref[g] / l_ref[g]
        gate = _sigmoid(gate_ref[g * qs:(g + 1) * qs, :].T)
        outs = []
        for h in range(H):
            c = h * N_BRANCH
            cols = slice(h * qs, (h + 1) * qs)
            outs.append(gate[c:c + 1, :] * o_cmp[g][:, cols] + gate[c + 1:c + 2, :] * o_sel[:, cols]
                        + gate[c + 2:c + 3, :] * o_win[g][:, cols])
        o_ref[g * qs:(g + 1) * qs, :] = jnp.concatenate(outs, axis=0).T


def _cmp_to_slc_t(n_cmp_pad, n_blk):
    n_cmp = n_cmp_pad - 1
    cs = np.arange(n_cmp) * CMP_STRIDE
    ss = np.arange(n_blk) * SLC_BLOCK
    ov = np.minimum(cs[:, None] + CMP_BLOCK, ss[None, :] + SLC_BLOCK) - np.maximum(cs[:, None], ss[None, :])
    m = np.zeros((n_cmp_pad, n_blk), np.float32)
    m[:n_cmp] = np.clip(ov, 0, None) / CMP_STRIDE
    return jnp.asarray(m.T, dtype=BF16)


def _nsa_attention(qp, qr, kc, vc, ks, vs, kw, vw, gates, batch, seq, tq, qs, tk):
    n_blk = seq // SLC_BLOCK
    n_cmp_pad = kc.shape[1]
    n_top = min(N_SLC, n_blk)
    win_len = min(WINDOW + qs, seq)
    tpb = seq // tq
    G = tq // qs
    W = N_GROUP_HEADS * qs
    full = lambda shape: pl.BlockSpec((1,) + shape, lambda b, i: (b, 0, 0))
    q_spec = pl.BlockSpec((1, GROUP_WIDTH, tq), lambda b, i: (b, 0, i))
    return pl.pallas_call(
        functools.partial(_nsa_attn_kernel, tq=tq, qs=qs, tk=tk, seq=seq, n_top=n_top, win_len=win_len),
        out_shape=jax.ShapeDtypeStruct((batch * seq, GROUP_WIDTH), F32),
        grid=(batch, tpb),
        in_specs=[q_spec, q_spec,
                  full((n_cmp_pad, HEAD_DIM)), full((HEAD_DIM, n_cmp_pad)),
                  full((seq, HEAD_DIM)), full((HEAD_DIM, seq)),
                  full((seq, HEAD_DIM)), full((HEAD_DIM, seq)),
                  pl.BlockSpec((tq, LANE), lambda b, i: (b * tpb + i, 0)),
                  pl.BlockSpec((n_blk, n_cmp_pad), lambda b, i: (0, 0))],
        out_specs=pl.BlockSpec((tq, GROUP_WIDTH), lambda b, i: (b * tpb + i, 0)),
        scratch_shapes=[pltpu.VMEM((G, n_blk, qs), F32), pltpu.VMEM((G, 1, W), F32),
                        pltpu.VMEM((G, 1, W), F32), pltpu.VMEM((G, HEAD_DIM, W), F32)],
        compiler_params=_cparams(("parallel", "parallel")),
        name="nsa_attn",
    )(qp, qr, kc, vc, ks, vs, kw, vw, gates, _cmp_to_slc_t(n_cmp_pad, n_blk))


def _gla_kernel(qk_ref, v_ref, g_ref, lr_ref, w2_ref, gb_ref, o_ref, state_ref, *, n_chunks):
    H, C = N_GROUP_HEADS, GLA_CHUNK
    DKW = H * GLA_DK
    DVW = H * GLA_DV

    @pl.when(pl.program_id(1) == 0)
    def _():
        state_ref[...] = jnp.zeros(state_ref.shape, F32)

    tri = (lax.broadcasted_iota(jnp.int32, (C, C), 0)
           >= lax.broadcasted_iota(jnp.int32, (C, C), 1))
    tri_b = jnp.where(tri, 1.0, 0.0).astype(BF16)
    qhead = lax.broadcasted_iota(jnp.int32, (C, DKW), 1) // GLA_DK
    vhead = lax.broadcasted_iota(jnp.int32, (C, DVW), 1) // GLA_DV
    causal = jnp.concatenate([tri] * H, axis=0)
    state_mask = (lax.broadcasted_iota(jnp.int32, (DKW, DVW), 0) // GLA_DK
                  == lax.broadcasted_iota(jnp.int32, (DKW, DVW), 1) // GLA_DV)

    pre = _dot_f32(lr_ref[...], w2_ref[...]) + gb_ref[...]
    log_a = (jnp.minimum(pre, 0.0) - jnp.log1p(jnp.exp(-jnp.abs(pre)))) / GLA_TAU
    q_all = qk_ref[:, 0:DKW] * (GLA_DK ** -0.5)
    k_all = qk_ref[:, DKW:2 * DKW]

    o_intra, q_dec, kv_new, decay = [], [], [], []
    for ci in range(n_chunks):
        rows = slice(ci * C, (ci + 1) * C)
        b = _dot_rhs3(tri_b, log_a[rows])
        b_mid = b[C // 2:C // 2 + 1, :]
        b_last = b[C - 1:C, :]
        q, k, v = q_all[rows], k_all[rows], v_ref[rows, :].astype(BF16)
        q_in = q * jnp.exp(b - b_mid)
        k_in = k * jnp.exp(b_mid - b)
        q_stack = jnp.concatenate([jnp.where(qhead == h, q_in, 0.0) for h in range(H)], axis=0)
        att = jnp.where(causal, _dot_nt(q_stack, k_in), 0.0)
        o_stack = jnp.dot(att.astype(BF16), v, preferred_element_type=F32)
        o = jnp.where(vhead == 0, o_stack[0:C], 0.0)
        for h in range(1, H):
            o = o + jnp.where(vhead == h, o_stack[h * C:(h + 1) * C], 0.0)
        o_intra.append(o)
        q_dec.append((q * jnp.exp(b)).astype(BF16))
        k_out = k * jnp.exp(b_last - b)
        kv = jnp.dot(k_out.T.astype(BF16), v, preferred_element_type=F32)
        kv_new.append(jnp.where(state_mask, kv, 0.0))
        decay.append(jnp.exp(jnp.broadcast_to(b_last, (8, DKW))).T[:, 0:1])

    state = state_ref[...]
    o_chunks = []
    for ci in range(n_chunks):
        o_chunks.append(o_intra[ci] + jnp.dot(q_dec[ci], state.astype(BF16), preferred_element_type=F32))
        state = decay[ci] * state + kv_new[ci]
    state_ref[...] = state

    o = jnp.concatenate(o_chunks, axis=0)
    ms = _dot_lhs3(o * o, _group_mean_matrix(DVW, GLA_DV))
    gate = g_ref[...]
    o_ref[...] = gate * _sigmoid(gate) * (o * lax.rsqrt(ms + RMS_EPS))


def _gla(qk, v, g, lr, gate_w2_l, gate_b_l, batch, seq, tg):
    tpb = seq // tg
    row = lambda b, i: (b * tpb + i, 0)
    w2 = jnp.zeros((LANE, N_GROUP_HEADS * GLA_DK), F32).at[:GLA_RANK].set(gate_w2_l)
    return pl.pallas_call(
        functools.partial(_gla_kernel, n_chunks=tg // GLA_CHUNK),
        out_shape=jax.ShapeDtypeStruct((batch * seq, GROUP_WIDTH), F32),
        grid=(batch, tpb),
        in_specs=[pl.BlockSpec((tg, 2 * N_GROUP_HEADS * GLA_DK), row),
                  pl.BlockSpec((tg, GROUP_WIDTH), row),
                  pl.BlockSpec((tg, GROUP_WIDTH), row),
                  pl.BlockSpec((tg, LANE), row),
                  pl.BlockSpec((LANE, N_GROUP_HEADS * GLA_DK), lambda b, i: (0, 0)),
                  pl.BlockSpec((1, N_GROUP_HEADS * GLA_DK), lambda b, i: (0, 0))],
        out_specs=pl.BlockSpec((tg, GROUP_WIDTH), row),
        scratch_shapes=[pltpu.VMEM((N_GROUP_HEADS * GLA_DK, N_GROUP_HEADS * GLA_DV), F32)],
        compiler_params=_cparams(("parallel", "arbitrary")),
        name="gla",
    )(qk, v, g, lr, w2, gate_b_l.reshape(1, -1))


def _gmlp_kernel(uv_ref, w_ref, bias_ref, o_ref, *, n_chunks):
    T, G = GMLP_CHUNK, N_GROUP_HEADS
    gdim = GROUP_WIDTH // G
    gmean = _group_mean_matrix(GROUP_WIDTH, gdim)
    tri = (lax.broadcasted_iota(jnp.int32, (T, T), 0)
           >= lax.broadcasted_iota(jnp.int32, (T, T), 1))
    head = lax.broadcasted_iota(jnp.int32, (T, GROUP_WIDTH), 1) // gdim
    w = [jnp.where(tri, w_ref[g], 0.0).astype(BF16) for g in range(G)]
    for ci in range(n_chunks):
        rows = slice(ci * T, (ci + 1) * T)
        u = uv_ref[rows, 0:GROUP_WIDTH]
        v = uv_ref[rows, GROUP_WIDTH:2 * GROUP_WIDTH]
        d = v - _dot_lhs3(v, gmean)
        vn = (d * lax.rsqrt(_dot_lhs3(d * d, gmean) + RMS_EPS)).astype(BF16)
        mixed = bias_ref[...]
        for g in range(G):
            mixed = mixed + jnp.where(head == g, jnp.dot(w[g], vn, preferred_element_type=F32), 0.0)
        o_ref[rows, :] = u * mixed


def _gmlp(uv, gmlp_ws_l, gmlp_b_l, rows, tc):
    gdim = GROUP_WIDTH // N_GROUP_HEADS
    bias = jnp.repeat(gmlp_b_l.T, gdim, axis=1)
    return pl.pallas_call(
        functools.partial(_gmlp_kernel, n_chunks=tc // GMLP_CHUNK),
        out_shape=jax.ShapeDtypeStruct((rows, GROUP_WIDTH), F32),
        grid=(rows // tc,),
        in_specs=[pl.BlockSpec((tc, 2 * GROUP_WIDTH), lambda i: (i, 0)),
                  pl.BlockSpec((N_GROUP_HEADS, GMLP_CHUNK, GMLP_CHUNK), lambda i: (0, 0, 0)),
                  pl.BlockSpec((GMLP_CHUNK, GROUP_WIDTH), lambda i: (0, 0))],
        out_specs=pl.BlockSpec((tc, GROUP_WIDTH), lambda i: (i, 0)),
        compiler_params=_cparams(("parallel",)),
        name="gmlp",
    )(uv, gmlp_ws_l, bias)


def _outproj_kernel(conv_ref, halo_ref, cw_ref, nsa_ref, gla_ref, gmlp_ref, gain_ref, wo_ref,
                    x_ref, mod_ref, g_ref, o_ref, *, tpb):
    GW = GROUP_WIDTH
    tm = conv_ref.shape[0]
    z = conv_ref[:, 2 * GW:3 * GW] * conv_ref[:, 0:GW]
    zh = halo_ref[:, 2 * GW:3 * GW] * halo_ref[:, 0:GW]
    zh = jnp.where(pl.program_id(0) % tpb == 0, 0.0, zh)
    row = lax.broadcasted_iota(jnp.int32, (tm, GW), 0)
    y = cw_ref[CONV_WIDTH - 1:CONV_WIDTH, :] * z
    for back in range(1, CONV_WIDTH):
        shifted = pltpu.roll(z, back, axis=0)
        for r in range(back):
            shifted = jnp.where(row == r, zh[8 - back + r:8 - back + r + 1, :], shifted)
        y = y + cw_ref[CONV_WIDTH - 1 - back:CONV_WIDTH - back, :] * shifted
    groups = (conv_ref[:, GW:2 * GW] * y, nsa_ref[...], gla_ref[...], gmlp_ref[...])
    acc = jnp.zeros((tm, D_MODEL), F32)
    for k, o in enumerate(groups):
        n = (_rms(o) * gain_ref[k:k + 1, :]).astype(BF16)
        acc = acc + jnp.dot(n, wo_ref[k * GW:(k + 1) * GW, :], preferred_element_type=F32)
    o_ref[...] = x_ref[...] + mod_ref[0, 2:3, :] * (_rms(acc) * g_ref[1:2, :])


def _out_projection(conv, nsa, gla, gmlp, conv_w_l, grp_gain_l, wo_b, x2, mod_l, norm_g_l, seq, tm):
    rows = x2.shape[0]
    tpb = seq // tm
    hpt = tm // 8
    gw = lambda i: (i, 0)
    const = lambda i: (0, 0)
    return pl.pallas_call(
        functools.partial(_outproj_kernel, tpb=tpb),
        out_shape=jax.ShapeDtypeStruct((rows, D_MODEL), F32),
        grid=(rows // tm,),
        in_specs=[pl.BlockSpec((tm, 3 * GROUP_WIDTH), gw),
                  pl.BlockSpec((8, 3 * GROUP_WIDTH), lambda i: (jnp.maximum(i * hpt - 1, 0), 0)),
                  pl.BlockSpec((CONV_WIDTH, GROUP_WIDTH), const),
                  pl.BlockSpec((tm, GROUP_WIDTH), gw),
                  pl.BlockSpec((tm, GROUP_WIDTH), gw),
                  pl.BlockSpec((tm, GROUP_WIDTH), gw),
                  pl.BlockSpec((N_MIXERS, GROUP_WIDTH), const),
                  pl.BlockSpec((D_MODEL, D_MODEL), const),
                  pl.BlockSpec((tm, D_MODEL), gw),
                  pl.BlockSpec((1, N_MOD, D_MODEL), lambda i: (i // tpb, 0, 0)),
                  pl.BlockSpec((4, D_MODEL), const)],
        out_specs=pl.BlockSpec((tm, D_MODEL), gw),
        compiler_params=_cparams(("parallel",)),
        name="out_proj",
    )(conv, conv, conv_w_l, nsa, gla, gmlp, grp_gain_l.reshape(N_MIXERS, GROUP_WIDTH), wo_b,
      x2, mod_l, norm_g_l)


def _mlp_kernel(x_ref, mod_ref, g_ref, wu_ref, wd_ref, o_ref, *, tf):
    x = x_ref[...]
    h = _rms(x) * g_ref[2:3, :]
    hb = (h * (1.0 + mod_ref[0, 4:5, :]) + mod_ref[0, 3:4, :]).astype(BF16)
    acc = jnp.zeros(x.shape, F32)
    for f in range(D_FF // tf):
        u = jnp.dot(hb, wu_ref[:, f * tf:(f + 1) * tf], preferred_element_type=F32)
        u = jnp.maximum(u, 0.0)
        acc = acc + jnp.dot((u * u).astype(BF16), wd_ref[f * tf:(f + 1) * tf, :],
                            preferred_element_type=F32)
    o_ref[...] = x + mod_ref[0, 5:6, :] * (_rms(acc) * g_ref[3:4, :])


def _mlp(x2, mod_l, norm_g_l, wu_b, wd_b, seq, tm):
    rows = x2.shape[0]
    tpb = seq // tm
    const = lambda i: (0, 0)
    return pl.pallas_call(
        functools.partial(_mlp_kernel, tf=D_MODEL),
        out_shape=jax.ShapeDtypeStruct((rows, D_MODEL), F32),
        grid=(rows // tm,),
        in_specs=[pl.BlockSpec((tm, D_MODEL), lambda i: (i, 0)),
                  pl.BlockSpec((1, N_MOD, D_MODEL), lambda i: (i // tpb, 0, 0)),
                  pl.BlockSpec((4, D_MODEL), const),
                  pl.BlockSpec((D_MODEL, D_FF), const, pipeline_mode=pl.Buffered(1)),
                  pl.BlockSpec((D_FF, D_MODEL), const, pipeline_mode=pl.Buffered(1))],
        out_specs=pl.BlockSpec((tm, D_MODEL), lambda i: (i, 0)),
        compiler_params=_cparams(("parallel",)),
        name="mlp",
    )(x2, mod_l, norm_g_l, wu_b, wd_b)


def _pad_in_proj(w_in_l):
    offs = np.cumsum((0,) + (3 * GROUP_WIDTH, GROUP_WIDTH, 2 * N_BRANCH * HEAD_DIM, N_BRANCH * N_GROUP_HEADS,
                             2 * N_GROUP_HEADS * GLA_DK, GROUP_WIDTH, GROUP_WIDTH, GLA_RANK,
                             2 * GROUP_WIDTH))
    parts = []
    for k, w in enumerate(SEG_WIDTHS):
        seg = w_in_l[:, offs[k]:offs[k + 1]]
        parts.append(jnp.pad(seg, ((0, 0), (0, w - seg.shape[1]))))
    return jnp.concatenate(parts, axis=1).astype(BF16)


def kernel(x, c, positions, w_in, conv_w, cmp_pos, cmp_w1, cmp_w2, gla_gate_w2, gla_gate_b, gmlp_ws, gmlp_b, grp_gain, w_o, norm_g, w_mod, b_mod, w_up, w_down):
    batch, seq, _ = x.shape
    depth = w_in.shape[0]
    rows = batch * seq
    tm = min(512, seq)
    qs = min(128, seq)
    tq = min(2 * qs, seq)
    tk = min(512, seq)

    mod = _modulation(c, w_mod, b_mod).reshape(depth, batch, N_MOD, D_MODEL)
    cos_t, sin_t = _rope_tables(positions)
    x2 = x.reshape(rows, D_MODEL)
    for l in range(depth):
        conv, n_q, n_kv, n_g, l_qk, l_v, l_g, l_lr, m_uv = _in_projection(
            x2, mod[l], norm_g[l], _pad_in_proj(w_in[l]), seq, tm)
        qp, qr, ks, vs, kw, vw = _nsa_prep(n_q, n_kv, cos_t, sin_t, batch, seq, tm)
        kc, vc = _nsa_compress(n_kv, cmp_pos[l], cmp_w1[l], cmp_w2[l], batch, seq)
        nsa = _nsa_attention(qp, qr, kc, vc, ks, vs, kw, vw, n_g, batch, seq, tq, qs, tk)
        gla = _gla(l_qk, l_v, l_g, l_lr, gla_gate_w2[l], gla_gate_b[l], batch, seq, tm)
        gmlp = _gmlp(m_uv, gmlp_ws[l], gmlp_b[l], rows, tm)
        x2 = _out_projection(conv, nsa, gla, gmlp, conv_w[l], grp_gain[l], w_o[l].astype(BF16),
                             x2, mod[l], norm_g[l], seq, tm)
        x2 = _mlp(x2, mod[l], norm_g[l], w_up[l].astype(BF16), w_down[l].astype(BF16), seq, tm)
    return x2.reshape(batch, seq, D_MODEL)
```

```python
import functools

import numpy as np
import jax
import jax.numpy as jnp
from jax import lax
from jax.experimental import pallas as pl
from jax.experimental.pallas import tpu as pltpu

D_MODEL = 1024
N_MIXERS = 4
GROUP_WIDTH = D_MODEL // N_MIXERS
HEAD_DIM = 64
N_GROUP_HEADS = GROUP_WIDTH // HEAD_DIM
D_FF = 4 * D_MODEL
N_MOD = 6
RMS_EPS = 1e-6
NEG_INF = -1e30

ROPE_THETA = 500000.0
ROPE_DIM = HEAD_DIM // 4
ROPE_HALF = ROPE_DIM // 2

CONV_WIDTH = 3

CMP_BLOCK = 32
CMP_STRIDE = 16
CMP_HIDDEN = 2 * HEAD_DIM
SLC_BLOCK = 64
N_SLC = 16
N_INIT_BLOCKS = 1
N_LOCAL_BLOCKS = 2
FORCE_BONUS = 1e4
WINDOW = 512
N_BRANCH = 3
BLK_GROUP = 16
LOG2E = 1.4426950408889634

GLA_DK = HEAD_DIM // 2
GLA_DV = HEAD_DIM
GLA_RANK = 16
GLA_TAU = 16.0
GLA_CHUNK = 64

GMLP_CHUNK = 128

LANE = 128
VMEM_LIMIT = 48 * 1024 * 1024

F32 = jnp.float32
BF16 = jnp.bfloat16

SEG_WIDTHS = (3 * GROUP_WIDTH,
              GROUP_WIDTH,
              2 * N_BRANCH * HEAD_DIM,
              LANE,
              2 * N_GROUP_HEADS * GLA_DK,
              GROUP_WIDTH,
              GROUP_WIDTH,
              LANE,
              2 * GROUP_WIDTH)
SEG_DTYPES = (BF16, BF16, F32, BF16, BF16, BF16, BF16, BF16, BF16)
SEG_OFFS = tuple(int(v) for v in np.cumsum((0,) + SEG_WIDTHS))
D_IN_PAD = SEG_OFFS[-1]


def _cparams(sem):
    return pltpu.CompilerParams(dimension_semantics=sem, vmem_limit_bytes=VMEM_LIMIT)


def _bdot(a, b):
    return jnp.dot(a.astype(BF16), b.astype(BF16), preferred_element_type=F32)


def _dot_nt(a, b):
    return lax.dot_general(a.astype(BF16), b.astype(BF16), (((1,), (1,)), ((), ())),
                           preferred_element_type=F32)


def _split2(a):
    hi = a.astype(BF16)
    lo = (a - hi.astype(F32)).astype(BF16)
    return hi, lo


def _split3(a):
    hi = a.astype(BF16)
    r = a - hi.astype(F32)
    mid = r.astype(BF16)
    lo = (r - mid.astype(F32)).astype(BF16)
    return hi, mid, lo


def _dot_lhs3(a, b_exact):
    hi, mid, lo = _split3(a)
    b = b_exact.astype(BF16)
    d = lambda p: jnp.dot(p, b, preferred_element_type=F32)
    return d(hi) + d(mid) + d(lo)


def _dot_rhs3(a_exact, b):
    hi, mid, lo = _split3(b)
    a = a_exact.astype(BF16)
    d = lambda p: jnp.dot(a, p, preferred_element_type=F32)
    return d(hi) + d(mid) + d(lo)


def _dot_f32(a, b):
    ah, al = _split2(a)
    bh, bl = _split2(b)
    d = lambda p, q: jnp.dot(p, q, preferred_element_type=F32)
    return d(ah, bh) + d(al, bh) + d(ah, bl)


def _rms(x):
    return x * lax.rsqrt(jnp.mean(x * x, axis=-1, keepdims=True) + RMS_EPS)


def _sigmoid(x):
    return 1.0 / (1.0 + jnp.exp(-x))


def _group_mean_matrix(width, group):
    r = lax.broadcasted_iota(jnp.int32, (width, width), 0) // group
    c = lax.broadcasted_iota(jnp.int32, (width, width), 1) // group
    return jnp.where(r == c, 1.0 / group, 0.0).astype(BF16)


def _mod_kernel(c_ref, w_ref, b_ref, o_ref):
    c = c_ref[...]
    cond = c * _sigmoid(c)
    o_ref[0] = _dot_f32(cond, w_ref[0]) + b_ref[0]


def _modulation(c, w_mod, b_mod):
    depth, d, n = w_mod.shape
    b = c.shape[0]
    tn = D_MODEL
    return pl.pallas_call(
        _mod_kernel,
        out_shape=jax.ShapeDtypeStruct((depth, b, n), F32),
        grid=(depth, n // tn),
        in_specs=[pl.BlockSpec((b, d), lambda l, j: (0, 0)),
                  pl.BlockSpec((1, d, tn), lambda l, j: (l, 0, j)),
                  pl.BlockSpec((1, 1, tn), lambda l, j: (l, 0, j))],
        out_specs=pl.BlockSpec((1, b, tn), lambda l, j: (l, 0, j)),
        compiler_params=_cparams(("parallel", "parallel")),
        name="adaln_mod",
    )(c, w_mod, b_mod.reshape(depth, 1, n))


def _rope_table_kernel(pos_ref, freq_ref, sign_ref, cos_ref, sin_ref):
    ang = pos_ref[...].astype(F32) * freq_ref[...]
    cos_ref[...] = jnp.cos(ang)
    sin_ref[...] = jnp.sin(ang) * sign_ref[...]


def _rope_tables(positions):
    b, s = positions.shape
    rows = b * s
    tr = min(1024, rows)
    inv_freq = ROPE_THETA ** (-jnp.arange(0, ROPE_DIM, 2, dtype=F32) / ROPE_DIM)
    lane = np.arange(LANE) % HEAD_DIM
    freq = jnp.where(lane < ROPE_DIM, inv_freq[lane % ROPE_HALF], 0.0).astype(F32)
    sign = jnp.asarray(np.where(lane < ROPE_HALF, -1.0, 1.0), dtype=F32)
    shape = jax.ShapeDtypeStruct((rows, LANE), F32)
    return pl.pallas_call(
        _rope_table_kernel,
        out_shape=(shape, shape),
        grid=(rows // tr,),
        in_specs=[pl.BlockSpec((tr, 1), lambda i: (i, 0)),
                  pl.BlockSpec((1, LANE), lambda i: (0, 0)),
                  pl.BlockSpec((1, LANE), lambda i: (0, 0))],
        out_specs=(pl.BlockSpec((tr, LANE), lambda i: (i, 0)),
                   pl.BlockSpec((tr, LANE), lambda i: (i, 0))),
        compiler_params=_cparams(("parallel",)),
        name="rope_tables",
    )(positions.reshape(rows, 1), freq.reshape(1, -1), sign.reshape(1, -1))


def _inproj_kernel(x_ref, mod_ref, g_ref, w_ref, *out_refs):
    h = _rms(x_ref[...]) * g_ref[0:1, :]
    h = h * (1.0 + mod_ref[0, 1:2, :]) + mod_ref[0, 0:1, :]
    hb = h.astype(BF16)
    for k, o_ref in enumerate(out_refs):
        o_ref[...] = jnp.dot(hb, w_ref[:, SEG_OFFS[k]:SEG_OFFS[k + 1]],
                             preferred_element_type=F32).astype(o_ref.dtype)


def _in_projection(x2, mod_l, norm_g_l, w_pad, seq, tm):
    rows = x2.shape[0]
    tpb = seq // tm
    outs = tuple(jax.ShapeDtypeStruct((rows, w), dt) for w, dt in zip(SEG_WIDTHS, SEG_DTYPES))
    return pl.pallas_call(
        _inproj_kernel,
        out_shape=outs,
        grid=(rows // tm,),
        in_specs=[pl.BlockSpec((tm, D_MODEL), lambda i: (i, 0)),
                  pl.BlockSpec((1, N_MOD, D_MODEL), lambda i: (i // tpb, 0, 0)),
                  pl.BlockSpec((4, D_MODEL), lambda i: (0, 0)),
                  pl.BlockSpec((D_MODEL, D_IN_PAD), lambda i: (0, 0))],
        out_specs=tuple(pl.BlockSpec((tm, w), lambda i: (i, 0)) for w in SEG_WIDTHS),
        compiler_params=_cparams(("parallel",)),
        name="in_proj",
    )(x2, mod_l, norm_g_l, w_pad)


def _rope(x, cos, sin):
    w = x.shape[-1]
    lane = lax.broadcasted_iota(jnp.int32, x.shape, 1) % HEAD_DIM
    swapped = jnp.where(lane < ROPE_HALF,
                        pltpu.roll(x, w - ROPE_HALF, axis=1),
                        pltpu.roll(x, ROPE_HALF, axis=1))
    return x * cos + swapped * sin


def _nsa_prep_kernel(q_ref, kv_ref, cos_ref, sin_ref,
                     qp_ref, qr_ref, ks_ref, vs_ref, kw_ref, vw_ref):
    tp = cos_ref.shape[0]
    cos = cos_ref[...]
    sin = sin_ref[...]
    q = q_ref[...].astype(F32) * (HEAD_DIM ** -0.5 * LOG2E)
    qp_ref[0] = q.T.astype(BF16)
    reps = GROUP_WIDTH // LANE
    qr_ref[0] = _rope(q, jnp.concatenate([cos] * reps, axis=1),
                      jnp.concatenate([sin] * reps, axis=1)).T.astype(BF16)
    lane = lax.broadcasted_iota(jnp.int32, (tp, LANE), 1)
    is_k = lane < HEAD_DIM
    cos_kv = jnp.where(is_k, cos, 1.0)
    sin_kv = jnp.where(is_k, sin, 0.0)
    slc = _rope(kv_ref[:, LANE:2 * LANE], cos_kv, sin_kv)
    win = _rope(kv_ref[:, 2 * LANE:3 * LANE], cos_kv, sin_kv)
    blk = ((pl.program_id(1) * tp + lax.broadcasted_iota(jnp.int32, (tp, LANE), 0)) // SLC_BLOCK) % BLK_GROUP
    onehot = jnp.where(lane - HEAD_DIM == blk, 1.0, 0.0)
    ks_ref[0] = jnp.where(is_k, slc, onehot).astype(BF16)
    vs_ref[0] = slc.T[HEAD_DIM:, :].astype(BF16)
    kw_ref[0] = win[:, :HEAD_DIM].astype(BF16)
    vw_ref[0] = win.T[HEAD_DIM:, :].astype(BF16)


def _nsa_prep(q, kv, cos_t, sin_t, batch, seq, tp):
    tpb = seq // tp
    row = lambda b, i: (b * tpb + i, 0)
    qs = jax.ShapeDtypeStruct((batch, GROUP_WIDTH, seq), BF16)
    ks = jax.ShapeDtypeStruct((batch, seq, LANE), BF16)
    kw = jax.ShapeDtypeStruct((batch, seq, HEAD_DIM), BF16)
    vs = jax.ShapeDtypeStruct((batch, HEAD_DIM, seq), BF16)
    q_spec = pl.BlockSpec((1, GROUP_WIDTH, tp), lambda b, i: (b, 0, i))
    ks_spec = pl.BlockSpec((1, tp, LANE), lambda b, i: (b, i, 0))
    kw_spec = pl.BlockSpec((1, tp, HEAD_DIM), lambda b, i: (b, i, 0))
    v_spec = pl.BlockSpec((1, HEAD_DIM, tp), lambda b, i: (b, 0, i))
    return pl.pallas_call(
        _nsa_prep_kernel,
        out_shape=(qs, qs, ks, vs, kw, vs),
        grid=(batch, tpb),
        in_specs=[pl.BlockSpec((tp, GROUP_WIDTH), row),
                  pl.BlockSpec((tp, 2 * N_BRANCH * HEAD_DIM), row),
                  pl.BlockSpec((tp, LANE), row),
                  pl.BlockSpec((tp, LANE), row)],
        out_specs=(q_spec, q_spec, ks_spec, v_spec, kw_spec, v_spec),
        compiler_params=_cparams(("parallel", "parallel")),
        name="nsa_prep",
    )(q, kv, cos_t, sin_t)


def _nsa_compress_kernel(kv_ref, pos_ref, w1_ref, w2_ref, kc_ref, vc_ref, *, n_half):
    half = CMP_BLOCK // 2
    assert half == CMP_STRIDE
    chunks = [kv_ref[pl.ds(r, n_half, stride=CMP_STRIDE), :] for r in range(half)]
    outs = []
    for j in range(2):
        first = jnp.zeros((n_half, CMP_HIDDEN), F32)
        second = jnp.zeros((n_half, CMP_HIDDEN), F32)
        for r in range(half):
            t = chunks[r][:, j * HEAD_DIM:(j + 1) * HEAD_DIM]
            first += _bdot(t + pos_ref[j, r:r + 1, :],
                           w1_ref[j, r * HEAD_DIM:(r + 1) * HEAD_DIM, :])
            second += _bdot(t + pos_ref[j, half + r:half + r + 1, :],
                            w1_ref[j, (half + r) * HEAD_DIM:(half + r + 1) * HEAD_DIM, :])
        pre = first + pltpu.roll(second, n_half - 1, axis=0)
        row = lax.broadcasted_iota(jnp.int32, pre.shape, 0)
        pre = jnp.where(row < n_half - 1, pre, 0.0)
        hid = pre * _sigmoid(pre)
        outs.append(_bdot(hid, w2_ref[j]))
    kc_ref[0] = outs[0].astype(BF16)
    vc_ref[0] = jnp.concatenate(outs, axis=1).T[HEAD_DIM:, :].astype(BF16)


def _nsa_compress(kv, cmp_pos_l, cmp_w1_l, cmp_w2_l, batch, seq):
    n_half = seq // CMP_STRIDE
    return pl.pallas_call(
        functools.partial(_nsa_compress_kernel, n_half=n_half),
        out_shape=(jax.ShapeDtypeStruct((batch, n_half, HEAD_DIM), BF16),
                   jax.ShapeDtypeStruct((batch, HEAD_DIM, n_half), BF16)),
        grid=(batch,),
        in_specs=[pl.BlockSpec((seq, LANE), lambda b: (b, 0)),
                  pl.BlockSpec((2, CMP_BLOCK, HEAD_DIM), lambda b: (0, 0, 0)),
                  pl.BlockSpec((2, CMP_BLOCK * HEAD_DIM, CMP_HIDDEN), lambda b: (0, 0, 0)),
                  pl.BlockSpec((2, CMP_HIDDEN, HEAD_DIM), lambda b: (0, 0, 0))],
        out_specs=(pl.BlockSpec((1, n_half, HEAD_DIM), lambda b: (b, 0, 0)),
                   pl.BlockSpec((1, HEAD_DIM, n_half), lambda b: (b, 0, 0))),
        compiler_params=_cparams(("parallel",)),
        name="nsa_compress",
    )(kv, cmp_pos_l, cmp_w1_l, cmp_w2_l)


def _nsa_attn_kernel(qp_ref, qr_ref, kc_ref, vc_ref, ks_ref, vs_ref, kw_ref, vw_ref, gate_ref,
                     ovl_ref, o_ref, sel_ref, s_ref,
                     *, tq, qs, tk, seq, n_top, win_len):
    H = N_GROUP_HEADS
    W = H * qs
    G = tq // qs
    n_cmp_pad = kc_ref.shape[1]
    n_cmp = n_cmp_pad - 1
    n_blk = seq // SLC_BLOCK
    blk_per_tile = tk // SLC_BLOCK
    t_step = pl.program_id(1) * tq

    def heads_on_lanes(ref, g):
        return jnp.concatenate([ref[0, h * HEAD_DIM:(h + 1) * HEAD_DIM, g * qs:(g + 1) * qs]
                                for h in range(H)], axis=1)

    def tile_heads(a):
        return jnp.concatenate([a] * H, axis=1)

    t_sub = [t_step + g * qs for g in range(G)]
    q_rot = [heads_on_lanes(qr_ref, g) for g in range(G)]
    w_start = [pl.multiple_of(jnp.maximum(t0 + qs - win_len, 0), qs) for t0 in t_sub]
    s_cmp = [jnp.dot(kc_ref[0], heads_on_lanes(qp_ref, g), preferred_element_type=F32) for g in range(G)]
    s_win = [jnp.dot(kw_ref[0, pl.ds(w0, win_len), :], q, preferred_element_type=F32)
             for w0, q in zip(w_start, q_rot)]

    p_cmp = []
    for t0, s in zip(t_sub, s_cmp):
        n_c = lax.broadcasted_iota(jnp.int32, (n_cmp_pad, qs), 0)
        t_c = t0 + lax.broadcasted_iota(jnp.int32, (n_cmp_pad, qs), 1)
        bias_c = jnp.where((n_c * CMP_STRIDE + CMP_BLOCK - 1 <= t_c) & (n_c < n_cmp), 0.0, NEG_INF)
        s_c = s + tile_heads(bias_c)
        e_c = jnp.exp2(s_c - jnp.max(s_c, axis=0, keepdims=True))
        any_valid = jnp.where(t_c[0:1, :] >= CMP_BLOCK - 1, 1.0, 0.0)
        p_cmp.append(e_c * (tile_heads(any_valid) / jnp.sum(e_c, axis=0, keepdims=True)))
    o_cmp = [jnp.dot(vc_ref[0], p_c.astype(BF16), preferred_element_type=F32) for p_c in p_cmp]

    p_slc = []
    for p_c in p_cmp:
        p_sum = p_c[:, 0:qs]
        for h in range(1, H):
            p_sum = p_sum + p_c[:, h * qs:(h + 1) * qs]
        ps_hi, ps_lo = _split2(p_sum)
        p_slc.append(jnp.dot(ovl_ref[...], ps_hi, preferred_element_type=F32)
                     + jnp.dot(ovl_ref[...], ps_lo, preferred_element_type=F32))

    o_win = []
    for t0, w0, s in zip(t_sub, w_start, s_win):
        diff = (t0 + lax.broadcasted_iota(jnp.int32, (win_len, qs), 1)
                - w0 - lax.broadcasted_iota(jnp.int32, (win_len, qs), 0))
        s_w = s + tile_heads(jnp.where((diff >= 0) & (diff < WINDOW), 0.0, NEG_INF))
        e_w = jnp.exp2(s_w - jnp.max(s_w, axis=0, keepdims=True))
        o_w = jnp.dot(vw_ref[0, :, pl.ds(w0, win_len)], e_w.astype(BF16), preferred_element_type=F32)
        o_win.append(o_w / jnp.sum(e_w, axis=0, keepdims=True))

    for g, (t0, imp) in enumerate(zip(t_sub, p_slc)):
        j_s = lax.broadcasted_iota(jnp.int32, (n_blk, qs), 0)
        cur = (t0 + lax.broadcasted_iota(jnp.int32, (n_blk, qs), 1)) // SLC_BLOCK
        forced = (j_s < N_INIT_BLOCKS) | ((cur - j_s >= 0) & (cur - j_s < N_LOCAL_BLOCKS))
        score = jnp.where(j_s <= cur, imp + jnp.where(forced, FORCE_BONUS, 0.0), NEG_INF)
        rank = jnp.zeros((n_blk, qs), F32)
        for jp in range(n_blk):
            row = score[jp:jp + 1, :]
            rank = rank + jnp.where(j_s > jp, jnp.where(row >= score, 1.0, 0.0),
                                    jnp.where(row > score, 1.0, 0.0))
        sel_ref[g] = jnp.where(rank < n_top, 1.0, 0.0)

    pad_rows = jnp.zeros((LANE - HEAD_DIM - BLK_GROUP, W), BF16)

    def augmented_queries(g, group):
        rows = sel_ref[g, pl.ds(pl.multiple_of(group * BLK_GROUP, BLK_GROUP), BLK_GROUP), :]
        bias = tile_heads((rows - 1.0) * -NEG_INF).astype(BF16)
        return jnp.concatenate([q_rot[g], bias, pad_rows], axis=0)

    t_blk0 = t_step // SLC_BLOCK
    strip = 2 * qs
    n_strip = W // strip
    init = []
    t_al = pl.multiple_of(t_step, tq)
    s_diag = [jnp.dot(ks_ref[0, pl.ds(t_al, (g + 1) * qs), :], augmented_queries(g, t_blk0 // BLK_GROUP),
                      preferred_element_type=F32) for g in range(G)]
    for g in range(G):
        n_diag = (g + 1) * qs
        causal = (lax.broadcasted_iota(jnp.int32, (n_diag, qs), 0)
                  <= g * qs + lax.broadcasted_iota(jnp.int32, (n_diag, qs), 1))
        s = s_diag[g] + tile_heads(jnp.where(causal, 0.0, NEG_INF))
        m_new = jnp.max(s, axis=0, keepdims=True)
        p = jnp.exp2(s - m_new)
        l_new = jnp.sum(p, axis=0, keepdims=True)
        acc = jnp.dot(vs_ref[0, :, pl.ds(t_al, n_diag)], p.astype(BF16), preferred_element_type=F32)
        for c in range(n_strip):
            cols = slice(c * strip, (c + 1) * strip)
            init.append((m_new[:, cols], l_new[:, cols], acc[:, cols]))

    th = tk // 2

    def score_strips(kt, half):
        k0 = pl.multiple_of(kt * tk + half * th, th)
        keys = ks_ref[0, pl.ds(k0, th), :]
        for g in range(G):
            q_aug = augmented_queries(g, kt // (BLK_GROUP // blk_per_tile))
            for c in range(n_strip):
                s_ref[half, g * n_strip + c] = jnp.dot(keys, q_aug[:, c * strip:(c + 1) * strip],
                                                       preferred_element_type=F32)

    def softmax_strips(kt, half, stats):
        k0 = pl.multiple_of(kt * tk + half * th, th)
        vals = vs_ref[0, :, pl.ds(k0, th)]
        out = []
        for c, (m_old, l_old, acc_old) in enumerate(stats):
            s = s_ref[half, c]
            m_new = jnp.maximum(m_old, jnp.max(s, axis=0, keepdims=True))
            p = jnp.exp2(s - m_new)
            alpha = jnp.exp2(m_old - m_new)
            out.append((m_new, alpha * l_old + jnp.sum(p, axis=0, keepdims=True),
                        alpha * acc_old + jnp.dot(vals, p.astype(BF16), preferred_element_type=F32)))
        return tuple(out)

    n_tiles = t_step // tk

    def key_tile(kt, stats):
        score_strips(kt, 1)
        stats = softmax_strips(kt, 0, stats)
        score_strips(jnp.minimum(kt + 1, n_tiles - 1), 0)
        return softmax_strips(kt, 1, stats)

    score_strips(0, 0)
    final = lax.fori_loop(0, n_tiles, key_tile, tuple(init))

    for g in range(G):
        gate = _sigmoid(gate_ref[g * qs:(g + 1) * qs, :].astype(F32).T)
        o_sel = jnp.concatenate([acc_fin / l_fin for _, l_fin, acc_fin in final[g * n_strip:(g + 1) * n_strip]],
                                axis=1)
        outs = []
        for h in range(H):
            c = h * N_BRANCH
            cols = slice(h * qs, (h + 1) * qs)
            outs.append(gate[c:c + 1, :] * o_cmp[g][:, cols] + gate[c + 1:c + 2, :] * o_sel[:, cols]
                        + gate[c + 2:c + 3, :] * o_win[g][:, cols])
        o_ref[g * qs:(g + 1) * qs, :] = jnp.concatenate(outs, axis=0).T.astype(o_ref.dtype)


def _cmp_to_slc_t(n_cmp_pad, n_blk):
    n_cmp = n_cmp_pad - 1
    cs = np.arange(n_cmp) * CMP_STRIDE
    ss = np.arange(n_blk) * SLC_BLOCK
    ov = np.minimum(cs[:, None] + CMP_BLOCK, ss[None, :] + SLC_BLOCK) - np.maximum(cs[:, None], ss[None, :])
    m = np.zeros((n_cmp_pad, n_blk), np.float32)
    m[:n_cmp] = np.clip(ov, 0, None) / CMP_STRIDE
    return jnp.asarray(m.T, dtype=BF16)


def _nsa_attention(qp, qr, kc, vc, ks, vs, kw, vw, gates, batch, seq, tq, qs, tk):
    n_blk = seq // SLC_BLOCK
    n_cmp_pad = kc.shape[1]
    n_top = min(N_SLC, n_blk)
    win_len = min(WINDOW + qs, seq)
    tpb = seq // tq
    G = tq // qs
    W = N_GROUP_HEADS * qs
    full = lambda shape: pl.BlockSpec((1,) + shape, lambda b, i: (b, 0, 0))
    q_spec = pl.BlockSpec((1, GROUP_WIDTH, tq), lambda b, i: (b, 0, i))
    return pl.pallas_call(
        functools.partial(_nsa_attn_kernel, tq=tq, qs=qs, tk=tk, seq=seq, n_top=n_top, win_len=win_len),
        out_shape=jax.ShapeDtypeStruct((batch * seq, GROUP_WIDTH), BF16),
        grid=(batch, tpb),
        in_specs=[q_spec, q_spec,
                  full((n_cmp_pad, HEAD_DIM)), full((HEAD_DIM, n_cmp_pad)),
                  full((seq, LANE)), full((HEAD_DIM, seq)),
                  full((seq, HEAD_DIM)), full((HEAD_DIM, seq)),
                  pl.BlockSpec((tq, LANE), lambda b, i: (b * tpb + i, 0)),
                  pl.BlockSpec((n_blk, n_cmp_pad), lambda b, i: (0, 0))],
        out_specs=pl.BlockSpec((tq, GROUP_WIDTH), lambda b, i: (b * tpb + i, 0)),
        scratch_shapes=[pltpu.VMEM((G, n_blk, qs), F32),
                        pltpu.VMEM((2, W * G // (2 * qs), tk // 2, 2 * qs), F32)],
        compiler_params=_cparams(("parallel", "parallel")),
        name="nsa_attn",
    )(qp, qr, kc, vc, ks, vs, kw, vw, gates, _cmp_to_slc_t(n_cmp_pad, n_blk))


def _gla_kernel(qk_ref, v_ref, g_ref, lr_ref, w2_ref, gb_ref, o_ref, state_ref, *, n_chunks):
    H, C = N_GROUP_HEADS, GLA_CHUNK
    DKW = H * GLA_DK
    DVW = H * GLA_DV

    @pl.when(pl.program_id(1) == 0)
    def _():
        state_ref[...] = jnp.zeros(state_ref.shape, F32)

    tri = (lax.broadcasted_iota(jnp.int32, (C, C), 0)
           >= lax.broadcasted_iota(jnp.int32, (C, C), 1))
    tri_b = jnp.where(tri, 1.0, 0.0).astype(BF16)
    qhead = lax.broadcasted_iota(jnp.int32, (C, DKW), 1) // GLA_DK
    vhead = lax.broadcasted_iota(jnp.int32, (C, DVW), 1) // GLA_DV
    causal = jnp.concatenate([tri] * H, axis=0)
    state_mask = (lax.broadcasted_iota(jnp.int32, (DKW, DVW), 0) // GLA_DK
                  == lax.broadcasted_iota(jnp.int32, (DKW, DVW), 1) // GLA_DV)

    pre = _dot_rhs3(lr_ref[...], w2_ref[...]) + gb_ref[...]
    log_a = (jnp.minimum(pre, 0.0) - jnp.log1p(jnp.exp(-jnp.abs(pre)))) / GLA_TAU
    q_all = qk_ref[:, 0:DKW].astype(F32) * (GLA_DK ** -0.5)
    k_all = qk_ref[:, DKW:2 * DKW].astype(F32)

    chunk_rows = [slice(ci * C, (ci + 1) * C) for ci in range(n_chunks)]
    bf = lambda a, w: jnp.dot(a, w, preferred_element_type=F32)
    la_hi, la_mid, la_lo = _split3(log_a)
    b_all = [bf(tri_b, la_hi[r]) + bf(tri_b, la_mid[r]) + bf(tri_b, la_lo[r]) for r in chunk_rows]
    v_all = [v_ref[r, :] for r in chunk_rows]
    q_stack, k_in, q_dec, k_out_t, decay = [], [], [], [], []
    for r, b in zip(chunk_rows, b_all):
        b_mid = b[C // 2:C // 2 + 1, :]
        b_last = b[C - 1:C, :]
        q, k = q_all[r], k_all[r]
        q_in = q * jnp.exp(b - b_mid)
        q_stack.append(jnp.concatenate([jnp.where(qhead == h, q_in, 0.0) for h in range(H)],
                                       axis=0).astype(BF16))
        k_in.append((k * jnp.exp(b_mid - b)).astype(BF16))
        q_dec.append((q * jnp.exp(b)).astype(BF16))
        k_out_t.append((k * jnp.exp(b_last - b)).T.astype(BF16))
        decay.append(jnp.exp(jnp.broadcast_to(b_last, (8, DKW))).T[:, 0:1])
    att = [lax.dot_general(qs_, ki_, (((1,), (1,)), ((), ())), preferred_element_type=F32)
           for qs_, ki_ in zip(q_stack, k_in)]
    att = [jnp.where(causal, a, 0.0).astype(BF16) for a in att]
    o_stack = [bf(a, v) for a, v in zip(att, v_all)]
    kv_new = [jnp.where(state_mask, bf(kt_, v), 0.0) for kt_, v in zip(k_out_t, v_all)]
    o_intra = []
    for os_ in o_stack:
        o = jnp.where(vhead == 0, os_[0:C], 0.0)
        for h in range(1, H):
            o = o + jnp.where(vhead == h, os_[h * C:(h + 1) * C], 0.0)
        o_intra.append(o)

    state = state_ref[...]
    o_chunks = []
    for ci in range(n_chunks):
        o_chunks.append(o_intra[ci] + bf(q_dec[ci], state.astype(BF16)))
        state = decay[ci] * state + kv_new[ci]
    state_ref[...] = state

    o = jnp.concatenate(o_chunks, axis=0)
    ms = _dot_lhs3(o * o, _group_mean_matrix(DVW, GLA_DV))
    gate = g_ref[...].astype(F32)
    o_ref[...] = (gate * _sigmoid(gate) * (o * lax.rsqrt(ms + RMS_EPS))).astype(o_ref.dtype)


def _gla(qk, v, g, lr, gate_w2_l, gate_b_l, batch, seq, tg):
    tpb = seq // tg
    row = lambda b, i: (b * tpb + i, 0)
    w2 = jnp.zeros((LANE, N_GROUP_HEADS * GLA_DK), F32).at[:GLA_RANK].set(gate_w2_l)
    return pl.pallas_call(
        functools.partial(_gla_kernel, n_chunks=tg // GLA_CHUNK),
        out_shape=jax.ShapeDtypeStruct((batch * seq, GROUP_WIDTH), BF16),
        grid=(batch, tpb),
        in_specs=[pl.BlockSpec((tg, 2 * N_GROUP_HEADS * GLA_DK), row),
                  pl.BlockSpec((tg, GROUP_WIDTH), row),
                  pl.BlockSpec((tg, GROUP_WIDTH), row),
                  pl.BlockSpec((tg, LANE), row),
                  pl.BlockSpec((LANE, N_GROUP_HEADS * GLA_DK), lambda b, i: (0, 0)),
                  pl.BlockSpec((1, N_GROUP_HEADS * GLA_DK), lambda b, i: (0, 0))],
        out_specs=pl.BlockSpec((tg, GROUP_WIDTH), row),
        scratch_shapes=[pltpu.VMEM((N_GROUP_HEADS * GLA_DK, N_GROUP_HEADS * GLA_DV), F32)],
        compiler_params=_cparams(("parallel", "arbitrary")),
        name="gla",
    )(qk, v, g, lr, w2, gate_b_l.reshape(1, -1))


def _gmlp_kernel(uv_ref, w_ref, bias_ref, o_ref, *, n_chunks):
    T, G = GMLP_CHUNK, N_GROUP_HEADS
    gdim = GROUP_WIDTH // G
    gmean = _group_mean_matrix(GROUP_WIDTH, gdim)
    tri = (lax.broadcasted_iota(jnp.int32, (T, T), 0)
           >= lax.broadcasted_iota(jnp.int32, (T, T), 1))
    head = lax.broadcasted_iota(jnp.int32, (T, GROUP_WIDTH), 1) // gdim
    w = [jnp.where(tri, w_ref[g], 0.0).astype(BF16) for g in range(G)]
    v = uv_ref[:, GROUP_WIDTH:2 * GROUP_WIDTH]
    d = v.astype(F32) - jnp.dot(v, gmean, preferred_element_type=F32)
    vn = (d * lax.rsqrt(_dot_lhs3(d * d, gmean) + RMS_EPS)).astype(BF16)
    mixes = [[jnp.dot(w[g], vn[ci * T:(ci + 1) * T], preferred_element_type=F32) for g in range(G)]
             for ci in range(n_chunks)]
    for ci in range(n_chunks):
        rows = slice(ci * T, (ci + 1) * T)
        mixed = bias_ref[...]
        for g in range(G):
            mixed = mixed + jnp.where(head == g, mixes[ci][g], 0.0)
        o_ref[rows, :] = (uv_ref[rows, 0:GROUP_WIDTH].astype(F32) * mixed).astype(o_ref.dtype)


def _gmlp(uv, gmlp_ws_l, gmlp_b_l, rows, tc):
    gdim = GROUP_WIDTH // N_GROUP_HEADS
    bias = jnp.repeat(gmlp_b_l.T, gdim, axis=1)
    return pl.pallas_call(
        functools.partial(_gmlp_kernel, n_chunks=tc // GMLP_CHUNK),
        out_shape=jax.ShapeDtypeStruct((rows, GROUP_WIDTH), BF16),
        grid=(rows // tc,),
        in_specs=[pl.BlockSpec((tc, 2 * GROUP_WIDTH), lambda i: (i, 0)),
                  pl.BlockSpec((N_GROUP_HEADS, GMLP_CHUNK, GMLP_CHUNK), lambda i: (0, 0, 0)),
                  pl.BlockSpec((GMLP_CHUNK, GROUP_WIDTH), lambda i: (0, 0))],
        out_specs=pl.BlockSpec((tc, GROUP_WIDTH), lambda i: (i, 0)),
        compiler_params=_cparams(("parallel",)),
        name="gmlp",
    )(uv, gmlp_ws_l, bias)


def _outproj_kernel(conv_ref, halo_ref, cw_ref, nsa_ref, gla_ref, gmlp_ref, gain_ref, wo_ref,
                    x_ref, mod_ref, g_ref, o_ref, *, tpb):
    GW = GROUP_WIDTH
    tm = conv_ref.shape[0]
    nh = halo_ref.shape[0]
    z = conv_ref[:, 2 * GW:3 * GW].astype(F32) * conv_ref[:, 0:GW].astype(F32)
    zh = halo_ref[:, 2 * GW:3 * GW].astype(F32) * halo_ref[:, 0:GW].astype(F32)
    zh = jnp.where(pl.program_id(0) % tpb == 0, 0.0, zh)
    row = lax.broadcasted_iota(jnp.int32, (tm, GW), 0)
    y = cw_ref[CONV_WIDTH - 1:CONV_WIDTH, :] * z
    for back in range(1, CONV_WIDTH):
        shifted = pltpu.roll(z, back, axis=0)
        for r in range(back):
            shifted = jnp.where(row == r, zh[nh - back + r:nh - back + r + 1, :], shifted)
        y = y + cw_ref[CONV_WIDTH - 1 - back:CONV_WIDTH - back, :] * shifted
    groups = (conv_ref[:, GW:2 * GW].astype(F32) * y, nsa_ref[...], gla_ref[...], gmlp_ref[...])
    acc = jnp.zeros((tm, D_MODEL), F32)
    for k, o in enumerate(groups):
        n = (_rms(o.astype(F32)) * gain_ref[k:k + 1, :]).astype(BF16)
        acc = acc + jnp.dot(n, wo_ref[k * GW:(k + 1) * GW, :], preferred_element_type=F32)
    o_ref[...] = x_ref[...] + mod_ref[0, 2:3, :] * (_rms(acc) * g_ref[1:2, :])


def _out_projection(conv, nsa, gla, gmlp, conv_w_l, grp_gain_l, wo_b, x2, mod_l, norm_g_l, seq, tm):
    rows = x2.shape[0]
    tpb = seq // tm
    halo = 16
    hpt = tm // halo
    gw = lambda i: (i, 0)
    const = lambda i: (0, 0)
    return pl.pallas_call(
        functools.partial(_outproj_kernel, tpb=tpb),
        out_shape=jax.ShapeDtypeStruct((rows, D_MODEL), F32),
        grid=(rows // tm,),
        in_specs=[pl.BlockSpec((tm, 3 * GROUP_WIDTH), gw),
                  pl.BlockSpec((halo, 3 * GROUP_WIDTH), lambda i: (jnp.maximum(i * hpt - 1, 0), 0)),
                  pl.BlockSpec((CONV_WIDTH, GROUP_WIDTH), const),
                  pl.BlockSpec((tm, GROUP_WIDTH), gw),
                  pl.BlockSpec((tm, GROUP_WIDTH), gw),
                  pl.BlockSpec((tm, GROUP_WIDTH), gw),
                  pl.BlockSpec((N_MIXERS, GROUP_WIDTH), const),
                  pl.BlockSpec((D_MODEL, D_MODEL), const),
                  pl.BlockSpec((tm, D_MODEL), gw),
                  pl.BlockSpec((1, N_MOD, D_MODEL), lambda i: (i // tpb, 0, 0)),
                  pl.BlockSpec((4, D_MODEL), const)],
        out_specs=pl.BlockSpec((tm, D_MODEL), gw),
        compiler_params=_cparams(("parallel",)),
        name="out_proj",
    )(conv, conv, conv_w_l, nsa, gla, gmlp, grp_gain_l.reshape(N_MIXERS, GROUP_WIDTH), wo_b,
      x2, mod_l, norm_g_l)


def _mlp_kernel(x_ref, mod_ref, g_ref, wu_ref, wd_ref, o_ref, *, tf):
    x = x_ref[...]
    h = _rms(x) * g_ref[2:3, :]
    hb = (h * (1.0 + mod_ref[0, 4:5, :]) + mod_ref[0, 3:4, :]).astype(BF16)
    acc = jnp.zeros(x.shape, F32)
    for f in range(D_FF // tf):
        u = jnp.dot(hb, wu_ref[:, f * tf:(f + 1) * tf], preferred_element_type=F32)
        u = jnp.maximum(u, 0.0)
        acc = acc + jnp.dot((u * u).astype(BF16), wd_ref[f * tf:(f + 1) * tf, :],
                            preferred_element_type=F32)
    o_ref[...] = x + mod_ref[0, 5:6, :] * (_rms(acc) * g_ref[3:4, :])


def _mlp(x2, mod_l, norm_g_l, wu_b, wd_b, seq, tm):
    rows = x2.shape[0]
    tpb = seq // tm
    const = lambda i: (0, 0)
    return pl.pallas_call(
        functools.partial(_mlp_kernel, tf=D_MODEL),
        out_shape=jax.ShapeDtypeStruct((rows, D_MODEL), F32),
        grid=(rows // tm,),
        in_specs=[pl.BlockSpec((tm, D_MODEL), lambda i: (i, 0)),
                  pl.BlockSpec((1, N_MOD, D_MODEL), lambda i: (i // tpb, 0, 0)),
                  pl.BlockSpec((4, D_MODEL), const),
                  pl.BlockSpec((D_MODEL, D_FF), const, pipeline_mode=pl.Buffered(1)),
                  pl.BlockSpec((D_FF, D_MODEL), const, pipeline_mode=pl.Buffered(1))],
        out_specs=pl.BlockSpec((tm, D_MODEL), lambda i: (i, 0)),
        compiler_params=_cparams(("parallel",)),
        name="mlp",
    )(x2, mod_l, norm_g_l, wu_b, wd_b)


def _pad_in_proj(w_in_l):
    offs = np.cumsum((0,) + (3 * GROUP_WIDTH, GROUP_WIDTH, 2 * N_BRANCH * HEAD_DIM, N_BRANCH * N_GROUP_HEADS,
                             2 * N_GROUP_HEADS * GLA_DK, GROUP_WIDTH, GROUP_WIDTH, GLA_RANK,
                             2 * GROUP_WIDTH))
    parts = []
    for k, w in enumerate(SEG_WIDTHS):
        seg = w_in_l[:, offs[k]:offs[k + 1]]
        parts.append(jnp.pad(seg, ((0, 0), (0, w - seg.shape[1]))))
    return jnp.concatenate(parts, axis=1).astype(BF16)


def kernel(x, c, positions, w_in, conv_w, cmp_pos, cmp_w1, cmp_w2, gla_gate_w2, gla_gate_b, gmlp_ws, gmlp_b, grp_gain, w_o, norm_g, w_mod, b_mod, w_up, w_down):
    batch, seq, _ = x.shape
    depth = w_in.shape[0]
    rows = batch * seq
    tm = min(512, seq)
    qs = min(128, seq)
    tq = min(2 * qs, seq)
    tk = min(256, seq)

    mod = _modulation(c, w_mod, b_mod).reshape(depth, batch, N_MOD, D_MODEL)
    cos_t, sin_t = _rope_tables(positions)
    x2 = x.reshape(rows, D_MODEL)
    for l in range(depth):
        conv, n_q, n_kv, n_g, l_qk, l_v, l_g, l_lr, m_uv = _in_projection(
            x2, mod[l], norm_g[l], _pad_in_proj(w_in[l]), seq, tm)
        qp, qr, ks, vs, kw, vw = _nsa_prep(n_q, n_kv, cos_t, sin_t, batch, seq, tm)
        kc, vc = _nsa_compress(n_kv, cmp_pos[l], cmp_w1[l], cmp_w2[l], batch, seq)
        nsa = _nsa_attention(qp, qr, kc, vc, ks, vs, kw, vw, n_g, batch, seq, tq, qs, tk)
        gla = _gla(l_qk, l_v, l_g, l_lr, gla_gate_w2[l], gla_gate_b[l], batch, seq, tm)
        gmlp = _gmlp(m_uv, gmlp_ws[l], gmlp_b[l], rows, tm)
        x2 = _out_projection(conv, nsa, gla, gmlp, conv_w[l], grp_gain[l], w_o[l].astype(BF16),
                             x2, mod[l], norm_g[l], seq, tm)
        x2 = _mlp(x2, mod[l], norm_g[l], w_up[l].astype(BF16), w_down[l].astype(BF16), seq, tm)
    return x2.reshape(batch, seq, D_MODEL)
```

```python
import functools

import numpy as np
import jax
import jax.numpy as jnp
from jax import lax
from jax.experimental import pallas as pl
from jax.experimental.pallas import tpu as pltpu

D_MODEL = 1024
N_MIXERS = 4
GROUP_WIDTH = D_MODEL // N_MIXERS
HEAD_DIM = 64
N_GROUP_HEADS = GROUP_WIDTH // HEAD_DIM
D_FF = 4 * D_MODEL
N_MOD = 6
RMS_EPS = 1e-6
NEG_INF = -1e30

ROPE_THETA = 500000.0
ROPE_DIM = HEAD_DIM // 4
ROPE_HALF = ROPE_DIM // 2

CONV_WIDTH = 3

CMP_BLOCK = 32
CMP_STRIDE = 16
CMP_HIDDEN = 2 * HEAD_DIM
SLC_BLOCK = 64
N_SLC = 16
N_INIT_BLOCKS = 1
N_LOCAL_BLOCKS = 2
FORCE_BONUS = 1e4
WINDOW = 512
N_BRANCH = 3
BLK_GROUP = 16
LOG2E = 1.4426950408889634

GLA_DK = HEAD_DIM // 2
GLA_DV = HEAD_DIM
GLA_RANK = 16
GLA_TAU = 16.0
GLA_CHUNK = 64

GMLP_CHUNK = 128

LANE = 128
VMEM_LIMIT = 48 * 1024 * 1024

F32 = jnp.float32
BF16 = jnp.bfloat16

N_GATE_COLS = N_BRANCH * N_GROUP_HEADS
SEG_WIDTHS = (3 * GROUP_WIDTH,
              GROUP_WIDTH,
              2 * N_BRANCH * HEAD_DIM,
              LANE,
              2 * N_GROUP_HEADS * GLA_DK,
              GROUP_WIDTH,
              GROUP_WIDTH,
              2 * GROUP_WIDTH)
SEG_DTYPES = (BF16, BF16, F32, BF16, BF16, BF16, BF16, BF16)
SEG_OFFS = tuple(int(v) for v in np.cumsum((0,) + SEG_WIDTHS))
D_IN_PAD = SEG_OFFS[-1]
DOT_GROUPS = ((0,), (1,), (2, 3), (4,), (5,), (6,), (7,))


def _cparams(sem):
    return pltpu.CompilerParams(dimension_semantics=sem, vmem_limit_bytes=VMEM_LIMIT)


def _bdot(a, b):
    return jnp.dot(a.astype(BF16), b.astype(BF16), preferred_element_type=F32)


def _dot_nt(a, b):
    return lax.dot_general(a.astype(BF16), b.astype(BF16), (((1,), (1,)), ((), ())),
                           preferred_element_type=F32)


def _split2(a):
    hi = a.astype(BF16)
    lo = (a - hi.astype(F32)).astype(BF16)
    return hi, lo


def _split3(a):
    hi = a.astype(BF16)
    r = a - hi.astype(F32)
    mid = r.astype(BF16)
    lo = (r - mid.astype(F32)).astype(BF16)
    return hi, mid, lo


def _dot_lhs3(a, b_exact):
    hi, mid, lo = _split3(a)
    b = b_exact.astype(BF16)
    d = lambda p: jnp.dot(p, b, preferred_element_type=F32)
    return d(hi) + d(mid) + d(lo)


def _dot_rhs3(a_exact, b):
    hi, mid, lo = _split3(b)
    a = a_exact.astype(BF16)
    d = lambda p: jnp.dot(a, p, preferred_element_type=F32)
    return d(hi) + d(mid) + d(lo)


def _dot_f32(a, b):
    ah, al = _split2(a)
    bh, bl = _split2(b)
    d = lambda p, q: jnp.dot(p, q, preferred_element_type=F32)
    return d(ah, bh) + d(al, bh) + d(ah, bl)


def _rms(x):
    return x * lax.rsqrt(jnp.mean(x * x, axis=-1, keepdims=True) + RMS_EPS)


def _sigmoid(x):
    return 1.0 / (1.0 + jnp.exp(-x))


def _group_mean_matrix(width, group):
    r = lax.broadcasted_iota(jnp.int32, (width, width), 0) // group
    c = lax.broadcasted_iota(jnp.int32, (width, width), 1) // group
    return jnp.where(r == c, 1.0 / group, 0.0).astype(BF16)


def _mod_kernel(c_ref, w_ref, b_ref, o_ref):
    c = c_ref[...]
    cond = c * _sigmoid(c)
    o_ref[0] = _dot_f32(cond, w_ref[0]) + b_ref[0]


def _modulation(c, w_mod, b_mod):
    depth, d, n = w_mod.shape
    b = c.shape[0]
    tn = D_MODEL
    return pl.pallas_call(
        _mod_kernel,
        out_shape=jax.ShapeDtypeStruct((depth, b, n), F32),
        grid=(depth, n // tn),
        in_specs=[pl.BlockSpec((b, d), lambda l, j: (0, 0)),
                  pl.BlockSpec((1, d, tn), lambda l, j: (l, 0, j)),
                  pl.BlockSpec((1, 1, tn), lambda l, j: (l, 0, j))],
        out_specs=pl.BlockSpec((1, b, tn), lambda l, j: (l, 0, j)),
        compiler_params=_cparams(("parallel", "parallel")),
        name="adaln_mod",
    )(c, w_mod, b_mod.reshape(depth, 1, n))


def _rope_table_kernel(pos_ref, freq_ref, sign_ref, cos_ref, sin_ref):
    ang = pos_ref[...].astype(F32) * freq_ref[...]
    cos_ref[...] = jnp.cos(ang)
    sin_ref[...] = jnp.sin(ang) * sign_ref[...]


def _rope_tables(positions):
    b, s = positions.shape
    rows = b * s
    tr = min(1024, rows)
    inv_freq = ROPE_THETA ** (-jnp.arange(0, ROPE_DIM, 2, dtype=F32) / ROPE_DIM)
    lane = np.arange(LANE) % HEAD_DIM
    freq = jnp.where(lane < ROPE_DIM, inv_freq[lane % ROPE_HALF], 0.0).astype(F32)
    sign = jnp.asarray(np.where(lane < ROPE_HALF, -1.0, 1.0), dtype=F32)
    shape = jax.ShapeDtypeStruct((rows, LANE), F32)
    return pl.pallas_call(
        _rope_table_kernel,
        out_shape=(shape, shape),
        grid=(rows // tr,),
        in_specs=[pl.BlockSpec((tr, 1), lambda i: (i, 0)),
                  pl.BlockSpec((1, LANE), lambda i: (0, 0)),
                  pl.BlockSpec((1, LANE), lambda i: (0, 0))],
        out_specs=(pl.BlockSpec((tr, LANE), lambda i: (i, 0)),
                   pl.BlockSpec((tr, LANE), lambda i: (i, 0))),
        compiler_params=_cparams(("parallel",)),
        name="rope_tables",
    )(positions.reshape(rows, 1), freq.reshape(1, -1), sign.reshape(1, -1))


def _inproj_kernel(x_ref, mod_ref, g_ref, w_ref, *out_refs):
    h = _rms(x_ref[...]) * g_ref[0:1, :]
    h = h * (1.0 + mod_ref[0, 1:2, :]) + mod_ref[0, 0:1, :]
    hb = h.astype(BF16)
    for group in DOT_GROUPS:
        lo, hi = SEG_OFFS[group[0]], SEG_OFFS[group[-1] + 1]
        z = jnp.dot(hb, w_ref[0, :, lo:hi], preferred_element_type=F32)
        for k in group:
            o_ref = out_refs[k]
            o_ref[...] = z[:, SEG_OFFS[k] - lo:SEG_OFFS[k + 1] - lo].astype(o_ref.dtype)


def _in_projection(x2, mod_l, norm_g_l, w_packed, layer, seq, tm):
    rows = x2.shape[0]
    tpb = seq // tm
    outs = tuple(jax.ShapeDtypeStruct((rows, w), dt) for w, dt in zip(SEG_WIDTHS, SEG_DTYPES))
    return pl.pallas_call(
        _inproj_kernel,
        out_shape=outs,
        grid=(rows // tm,),
        in_specs=[pl.BlockSpec((tm, D_MODEL), lambda i: (i, 0)),
                  pl.BlockSpec((1, N_MOD, D_MODEL), lambda i: (i // tpb, 0, 0)),
                  pl.BlockSpec((4, D_MODEL), lambda i: (0, 0)),
                  pl.BlockSpec((1, D_MODEL, D_IN_PAD), lambda i: (layer, 0, 0))],
        out_specs=tuple(pl.BlockSpec((tm, w), lambda i: (i, 0)) for w in SEG_WIDTHS),
        compiler_params=_cparams(("parallel",)),
        name="in_proj",
    )(x2, mod_l, norm_g_l, w_packed)


def _rope(x, cos, sin):
    w = x.shape[-1]
    lane = lax.broadcasted_iota(jnp.int32, x.shape, 1) % HEAD_DIM
    swapped = jnp.where(lane < ROPE_HALF,
                        pltpu.roll(x, w - ROPE_HALF, axis=1),
                        pltpu.roll(x, ROPE_HALF, axis=1))
    return x * cos + swapped * sin


def _nsa_prep_kernel(q_ref, kv_ref, cos_ref, sin_ref,
                     qp_ref, qr_ref, ks_ref, vs_ref, kw_ref, vw_ref):
    tp = cos_ref.shape[0]
    cos = cos_ref[...]
    sin = sin_ref[...]
    q = q_ref[...].astype(F32) * (HEAD_DIM ** -0.5 * LOG2E)
    qp_ref[0] = q.T.astype(BF16)
    reps = GROUP_WIDTH // LANE
    qr_ref[0] = _rope(q, jnp.concatenate([cos] * reps, axis=1),
                      jnp.concatenate([sin] * reps, axis=1)).T.astype(BF16)
    lane = lax.broadcasted_iota(jnp.int32, (tp, LANE), 1)
    is_k = lane < HEAD_DIM
    cos_kv = jnp.where(is_k, cos, 1.0)
    sin_kv = jnp.where(is_k, sin, 0.0)
    slc = _rope(kv_ref[:, LANE:2 * LANE], cos_kv, sin_kv)
    win = _rope(kv_ref[:, 2 * LANE:3 * LANE], cos_kv, sin_kv)
    blk = ((pl.program_id(1) * tp + lax.broadcasted_iota(jnp.int32, (tp, LANE), 0)) // SLC_BLOCK) % BLK_GROUP
    onehot = jnp.where(lane - HEAD_DIM == blk, 1.0, 0.0)
    ks_ref[0] = jnp.where(is_k, slc, onehot).astype(BF16)
    vs_ref[0] = slc.T[HEAD_DIM:, :].astype(BF16)
    kw_ref[0] = win[:, :HEAD_DIM].astype(BF16)
    vw_ref[0] = win.T[HEAD_DIM:, :].astype(BF16)


def _nsa_prep(q, kv, cos_t, sin_t, batch, seq, tp):
    tpb = seq // tp
    row = lambda b, i: (b * tpb + i, 0)
    qs = jax.ShapeDtypeStruct((batch, GROUP_WIDTH, seq), BF16)
    ks = jax.ShapeDtypeStruct((batch, seq, LANE), BF16)
    kw = jax.ShapeDtypeStruct((batch, seq, HEAD_DIM), BF16)
    vs = jax.ShapeDtypeStruct((batch, HEAD_DIM, seq), BF16)
    q_spec = pl.BlockSpec((1, GROUP_WIDTH, tp), lambda b, i: (b, 0, i))
    ks_spec = pl.BlockSpec((1, tp, LANE), lambda b, i: (b, i, 0))
    kw_spec = pl.BlockSpec((1, tp, HEAD_DIM), lambda b, i: (b, i, 0))
    v_spec = pl.BlockSpec((1, HEAD_DIM, tp), lambda b, i: (b, 0, i))
    return pl.pallas_call(
        _nsa_prep_kernel,
        out_shape=(qs, qs, ks, vs, kw, vs),
        grid=(batch, tpb),
        in_specs=[pl.BlockSpec((tp, GROUP_WIDTH), row),
                  pl.BlockSpec((tp, 2 * N_BRANCH * HEAD_DIM), row),
                  pl.BlockSpec((tp, LANE), row),
                  pl.BlockSpec((tp, LANE), row)],
        out_specs=(q_spec, q_spec, ks_spec, v_spec, kw_spec, v_spec),
        compiler_params=_cparams(("parallel", "parallel")),
        name="nsa_prep",
    )(q, kv, cos_t, sin_t)


def _nsa_compress_kernel(kv_ref, pos_ref, w1_ref, w2_ref, kc_ref, vc_ref, *, n_half):
    half = CMP_BLOCK // 2
    assert half == CMP_STRIDE
    chunks = [kv_ref[pl.ds(r, n_half, stride=CMP_STRIDE), :] for r in range(half)]
    outs = []
    for j in range(2):
        first = jnp.zeros((n_half, CMP_HIDDEN), F32)
        second = jnp.zeros((n_half, CMP_HIDDEN), F32)
        for r in range(half):
            t = chunks[r][:, j * HEAD_DIM:(j + 1) * HEAD_DIM]
            first += _bdot(t + pos_ref[j, r:r + 1, :],
                           w1_ref[j, r * HEAD_DIM:(r + 1) * HEAD_DIM, :])
            second += _bdot(t + pos_ref[j, half + r:half + r + 1, :],
                            w1_ref[j, (half + r) * HEAD_DIM:(half + r + 1) * HEAD_DIM, :])
        pre = first + pltpu.roll(second, n_half - 1, axis=0)
        row = lax.broadcasted_iota(jnp.int32, pre.shape, 0)
        pre = jnp.where(row < n_half - 1, pre, 0.0)
        hid = pre * _sigmoid(pre)
        outs.append(_bdot(hid, w2_ref[j]))
    kc_ref[0] = outs[0].astype(BF16)
    vc_ref[0] = jnp.concatenate(outs, axis=1).T[HEAD_DIM:, :].astype(BF16)


def _nsa_compress(kv, cmp_pos_l, cmp_w1_l, cmp_w2_l, batch, seq):
    n_half = seq // CMP_STRIDE
    return pl.pallas_call(
        functools.partial(_nsa_compress_kernel, n_half=n_half),
        out_shape=(jax.ShapeDtypeStruct((batch, n_half, HEAD_DIM), BF16),
                   jax.ShapeDtypeStruct((batch, HEAD_DIM, n_half), BF16)),
        grid=(batch,),
        in_specs=[pl.BlockSpec((seq, LANE), lambda b: (b, 0)),
                  pl.BlockSpec((2, CMP_BLOCK, HEAD_DIM), lambda b: (0, 0, 0)),
                  pl.BlockSpec((2, CMP_BLOCK * HEAD_DIM, CMP_HIDDEN), lambda b: (0, 0, 0)),
                  pl.BlockSpec((2, CMP_HIDDEN, HEAD_DIM), lambda b: (0, 0, 0))],
        out_specs=(pl.BlockSpec((1, n_half, HEAD_DIM), lambda b: (b, 0, 0)),
                   pl.BlockSpec((1, HEAD_DIM, n_half), lambda b: (b, 0, 0))),
        compiler_params=_cparams(("parallel",)),
        name="nsa_compress",
    )(kv, cmp_pos_l, cmp_w1_l, cmp_w2_l)


def _nsa_attn_kernel(qp_ref, qr_ref, kc_ref, vc_ref, ks_ref, vs_ref, kw_ref, vw_ref, gate_ref,
                     ovl_ref, o_ref, sel_ref, s_ref,
                     *, tq, qs, tk, seq, n_top, win_len):
    H = N_GROUP_HEADS
    W = H * qs
    G = tq // qs
    n_cmp_pad = kc_ref.shape[1]
    n_cmp = n_cmp_pad - 1
    n_blk = seq // SLC_BLOCK
    blk_per_tile = tk // SLC_BLOCK
    t_step = pl.program_id(1) * tq

    def heads_on_lanes(ref, g):
        return jnp.concatenate([ref[0, h * HEAD_DIM:(h + 1) * HEAD_DIM, g * qs:(g + 1) * qs]
                                for h in range(H)], axis=1)

    def tile_heads(a):
        return jnp.concatenate([a] * H, axis=1)

    t_sub = [t_step + g * qs for g in range(G)]
    q_rot = [heads_on_lanes(qr_ref, g) for g in range(G)]
    w_start = [pl.multiple_of(jnp.maximum(t0 + qs - win_len, 0), qs) for t0 in t_sub]
    s_cmp = [jnp.dot(kc_ref[0], heads_on_lanes(qp_ref, g), preferred_element_type=F32) for g in range(G)]
    s_win = [jnp.dot(kw_ref[0, pl.ds(w0, win_len), :], q, preferred_element_type=F32)
             for w0, q in zip(w_start, q_rot)]

    p_cmp = []
    for t0, s in zip(t_sub, s_cmp):
        n_c = lax.broadcasted_iota(jnp.int32, (n_cmp_pad, qs), 0)
        t_c = t0 + lax.broadcasted_iota(jnp.int32, (n_cmp_pad, qs), 1)
        bias_c = jnp.where((n_c * CMP_STRIDE + CMP_BLOCK - 1 <= t_c) & (n_c < n_cmp), 0.0, NEG_INF)
        s_c = s + tile_heads(bias_c)
        e_c = jnp.exp2(s_c - jnp.max(s_c, axis=0, keepdims=True))
        any_valid = jnp.where(t_c[0:1, :] >= CMP_BLOCK - 1, 1.0, 0.0)
        p_cmp.append(e_c * (tile_heads(any_valid) / jnp.sum(e_c, axis=0, keepdims=True)))
    o_cmp = [jnp.dot(vc_ref[0], p_c.astype(BF16), preferred_element_type=F32) for p_c in p_cmp]

    p_slc = []
    for p_c in p_cmp:
        p_sum = p_c[:, 0:qs]
        for h in range(1, H):
            p_sum = p_sum + p_c[:, h * qs:(h + 1) * qs]
        ps_hi, ps_lo = _split2(p_sum)
        p_slc.append(jnp.dot(ovl_ref[...], ps_hi, preferred_element_type=F32)
                     + jnp.dot(ovl_ref[...], ps_lo, preferred_element_type=F32))

    o_win = []
    for t0, w0, s in zip(t_sub, w_start, s_win):
        diff = (t0 + lax.broadcasted_iota(jnp.int32, (win_len, qs), 1)
                - w0 - lax.broadcasted_iota(jnp.int32, (win_len, qs), 0))
        s_w = s + tile_heads(jnp.where((diff >= 0) & (diff < WINDOW), 0.0, NEG_INF))
        e_w = jnp.exp2(s_w - jnp.max(s_w, axis=0, keepdims=True))
        o_w = jnp.dot(vw_ref[0, :, pl.ds(w0, win_len)], e_w.astype(BF16), preferred_element_type=F32)
        o_win.append(o_w / jnp.sum(e_w, axis=0, keepdims=True))

    for g, (t0, imp) in enumerate(zip(t_sub, p_slc)):
        j_s = lax.broadcasted_iota(jnp.int32, (n_blk, qs), 0)
        cur = (t0 + lax.broadcasted_iota(jnp.int32, (n_blk, qs), 1)) // SLC_BLOCK
        forced = (j_s < N_INIT_BLOCKS) | ((cur - j_s >= 0) & (cur - j_s < N_LOCAL_BLOCKS))
        score = jnp.where(j_s <= cur, imp + jnp.where(forced, FORCE_BONUS, 0.0), NEG_INF)
        rank = jnp.zeros((n_blk, qs), F32)
        for jp in range(n_blk):
            row = score[jp:jp + 1, :]
            rank = rank + jnp.where(j_s > jp, jnp.where(row >= score, 1.0, 0.0),
                                    jnp.where(row > score, 1.0, 0.0))
        sel_ref[g] = jnp.where(rank < n_top, 1.0, 0.0)

    pad_rows = jnp.zeros((LANE - HEAD_DIM - BLK_GROUP, W), BF16)

    def augmented_queries(g, group):
        rows = sel_ref[g, pl.ds(pl.multiple_of(group * BLK_GROUP, BLK_GROUP), BLK_GROUP), :]
        bias = tile_heads((rows - 1.0) * -NEG_INF).astype(BF16)
        return jnp.concatenate([q_rot[g], bias, pad_rows], axis=0)

    t_blk0 = t_step // SLC_BLOCK
    strip = 2 * qs
    n_strip = W // strip
    init = []
    t_al = pl.multiple_of(t_step, tq)
    s_diag = [jnp.dot(ks_ref[0, pl.ds(t_al, (g + 1) * qs), :], augmented_queries(g, t_blk0 // BLK_GROUP),
                      preferred_element_type=F32) for g in range(G)]
    for g in range(G):
        n_diag = (g + 1) * qs
        causal = (lax.broadcasted_iota(jnp.int32, (n_diag, qs), 0)
                  <= g * qs + lax.broadcasted_iota(jnp.int32, (n_diag, qs), 1))
        s = s_diag[g] + tile_heads(jnp.where(causal, 0.0, NEG_INF))
        m_new = jnp.max(s, axis=0, keepdims=True)
        p = jnp.exp2(s - m_new)
        l_new = jnp.sum(p, axis=0, keepdims=True)
        acc = jnp.dot(vs_ref[0, :, pl.ds(t_al, n_diag)], p.astype(BF16), preferred_element_type=F32)
        for c in range(n_strip):
            cols = slice(c * strip, (c + 1) * strip)
            init.append((m_new[:, cols], l_new[:, cols], acc[:, cols]))

    th = tk // 2

    def score_strips(kt, half):
        k0 = pl.multiple_of(kt * tk + half * th, th)
        keys = ks_ref[0, pl.ds(k0, th), :]
        for g in range(G):
            q_aug = augmented_queries(g, kt // (BLK_GROUP // blk_per_tile))
            for c in range(n_strip):
                s_ref[half, g * n_strip + c] = jnp.dot(keys, q_aug[:, c * strip:(c + 1) * strip],
                                                       preferred_element_type=F32)

    def softmax_strips(kt, half, stats):
        k0 = pl.multiple_of(kt * tk + half * th, th)
        vals = vs_ref[0, :, pl.ds(k0, th)]
        out = []
        for c, (m_old, l_old, acc_old) in enumerate(stats):
            s = s_ref[half, c]
            m_new = jnp.maximum(m_old, jnp.max(s, axis=0, keepdims=True))
            p = jnp.exp2(s - m_new)
            alpha = jnp.exp2(m_old - m_new)
            out.append((m_new, alpha * l_old + jnp.sum(p, axis=0, keepdims=True),
                        alpha * acc_old + jnp.dot(vals, p.astype(BF16), preferred_element_type=F32)))
        return tuple(out)

    n_tiles = t_step // tk

    def key_tile(kt, stats):
        score_strips(kt, 1)
        stats = softmax_strips(kt, 0, stats)
        score_strips(jnp.minimum(kt + 1, n_tiles - 1), 0)
        return softmax_strips(kt, 1, stats)

    score_strips(0, 0)
    final = lax.fori_loop(0, n_tiles, key_tile, tuple(init))

    for g in range(G):
        gate = _sigmoid(gate_ref[g * qs:(g + 1) * qs, :].astype(F32).T)
        o_sel = jnp.concatenate([acc_fin / l_fin for _, l_fin, acc_fin in final[g * n_strip:(g + 1) * n_strip]],
                                axis=1)
        outs = []
        for h in range(H):
            c = h * N_BRANCH
            cols = slice(h * qs, (h + 1) * qs)
            outs.append(gate[c:c + 1, :] * o_cmp[g][:, cols] + gate[c + 1:c + 2, :] * o_sel[:, cols]
                        + gate[c + 2:c + 3, :] * o_win[g][:, cols])
        o_ref[g * qs:(g + 1) * qs, :] = jnp.concatenate(outs, axis=0).T.astype(o_ref.dtype)


def _cmp_to_slc_t(n_cmp_pad, n_blk):
    n_cmp = n_cmp_pad - 1
    cs = np.arange(n_cmp) * CMP_STRIDE
    ss = np.arange(n_blk) * SLC_BLOCK
    ov = np.minimum(cs[:, None] + CMP_BLOCK, ss[None, :] + SLC_BLOCK) - np.maximum(cs[:, None], ss[None, :])
    m = np.zeros((n_cmp_pad, n_blk), np.float32)
    m[:n_cmp] = np.clip(ov, 0, None) / CMP_STRIDE
    return jnp.asarray(m.T, dtype=BF16)


def _nsa_attention(qp, qr, kc, vc, ks, vs, kw, vw, gates, batch, seq, tq, qs, tk):
    n_blk = seq // SLC_BLOCK
    n_cmp_pad = kc.shape[1]
    n_top = min(N_SLC, n_blk)
    win_len = min(WINDOW + qs, seq)
    tpb = seq // tq
    G = tq // qs
    W = N_GROUP_HEADS * qs
    full = lambda shape: pl.BlockSpec((1,) + shape, lambda b, i: (b, 0, 0))
    q_spec = pl.BlockSpec((1, GROUP_WIDTH, tq), lambda b, i: (b, 0, i))
    return pl.pallas_call(
        functools.partial(_nsa_attn_kernel, tq=tq, qs=qs, tk=tk, seq=seq, n_top=n_top, win_len=win_len),
        out_shape=jax.ShapeDtypeStruct((batch * seq, GROUP_WIDTH), BF16),
        grid=(batch, tpb),
        in_specs=[q_spec, q_spec,
                  full((n_cmp_pad, HEAD_DIM)), full((HEAD_DIM, n_cmp_pad)),
                  full((seq, LANE)), full((HEAD_DIM, seq)),
                  full((seq, HEAD_DIM)), full((HEAD_DIM, seq)),
                  pl.BlockSpec((tq, LANE), lambda b, i: (b * tpb + i, 0)),
                  pl.BlockSpec((n_blk, n_cmp_pad), lambda b, i: (0, 0))],
        out_specs=pl.BlockSpec((tq, GROUP_WIDTH), lambda b, i: (b * tpb + i, 0)),
        scratch_shapes=[pltpu.VMEM((G, n_blk, qs), F32),
                        pltpu.VMEM((2, W * G // (2 * qs), tk // 2, 2 * qs), F32)],
        compiler_params=_cparams(("parallel", "parallel")),
        name="nsa_attn",
    )(qp, qr, kc, vc, ks, vs, kw, vw, gates, _cmp_to_slc_t(n_cmp_pad, n_blk))


def _gla_kernel(qk_ref, v_ref, g_ref, lr_ref, w2_ref, gb_ref, o_ref, state_ref, *, n_chunks):
    H, C = N_GROUP_HEADS, GLA_CHUNK
    DKW = H * GLA_DK
    DVW = H * GLA_DV

    @pl.when(pl.program_id(1) == 0)
    def _():
        state_ref[...] = jnp.zeros(state_ref.shape, F32)

    tri = (lax.broadcasted_iota(jnp.int32, (C, C), 0)
           >= lax.broadcasted_iota(jnp.int32, (C, C), 1))
    tri_b = jnp.where(tri, 1.0, 0.0).astype(BF16)
    qhead = lax.broadcasted_iota(jnp.int32, (C, DKW), 1) // GLA_DK
    vhead = lax.broadcasted_iota(jnp.int32, (C, DVW), 1) // GLA_DV
    causal = jnp.concatenate([tri] * H, axis=0)
    state_mask = (lax.broadcasted_iota(jnp.int32, (DKW, DVW), 0) // GLA_DK
                  == lax.broadcasted_iota(jnp.int32, (DKW, DVW), 1) // GLA_DV)

    pre = _dot_rhs3(lr_ref[...], w2_ref[...]) + gb_ref[...]
    log_a = (jnp.minimum(pre, 0.0) - jnp.log1p(jnp.exp(-jnp.abs(pre)))) / GLA_TAU
    q_all = qk_ref[:, 0:DKW].astype(F32) * (GLA_DK ** -0.5)
    k_all = qk_ref[:, DKW:2 * DKW].astype(F32)

    chunk_rows = [slice(ci * C, (ci + 1) * C) for ci in range(n_chunks)]
    bf = lambda a, w: jnp.dot(a, w, preferred_element_type=F32)
    la_hi, la_mid, la_lo = _split3(log_a)
    b_all = [bf(tri_b, la_hi[r]) + bf(tri_b, la_mid[r]) + bf(tri_b, la_lo[r]) for r in chunk_rows]
    v_all = [v_ref[r, :] for r in chunk_rows]
    q_stack, k_in, q_dec, k_out_t, decay = [], [], [], [], []
    for r, b in zip(chunk_rows, b_all):
        b_mid = b[C // 2:C // 2 + 1, :]
        b_last = b[C - 1:C, :]
        q, k = q_all[r], k_all[r]
        q_in = q * jnp.exp(b - b_mid)
        q_stack.append(jnp.concatenate([jnp.where(qhead == h, q_in, 0.0) for h in range(H)],
                                       axis=0).astype(BF16))
        k_in.append((k * jnp.exp(b_mid - b)).astype(BF16))
        q_dec.append((q * jnp.exp(b)).astype(BF16))
        k_out_t.append((k * jnp.exp(b_last - b)).T.astype(BF16))
        decay.append(jnp.exp(jnp.broadcast_to(b_last, (8, DKW))).T[:, 0:1])
    att = [lax.dot_general(qs_, ki_, (((1,), (1,)), ((), ())), preferred_element_type=F32)
           for qs_, ki_ in zip(q_stack, k_in)]
    att = [jnp.where(causal, a, 0.0).astype(BF16) for a in att]
    o_stack = [bf(a, v) for a, v in zip(att, v_all)]
    kv_new = [jnp.where(state_mask, bf(kt_, v), 0.0) for kt_, v in zip(k_out_t, v_all)]
    o_intra = []
    for os_ in o_stack:
        o = jnp.where(vhead == 0, os_[0:C], 0.0)
        for h in range(1, H):
            o = o + jnp.where(vhead == h, os_[h * C:(h + 1) * C], 0.0)
        o_intra.append(o)

    state = state_ref[...]
    o_chunks = []
    for ci in range(n_chunks):
        o_chunks.append(o_intra[ci] + bf(q_dec[ci], state.astype(BF16)))
        state = decay[ci] * state + kv_new[ci]
    state_ref[...] = state

    o = jnp.concatenate(o_chunks, axis=0)
    ms = _dot_lhs3(o * o, _group_mean_matrix(DVW, GLA_DV))
    gate = g_ref[...].astype(F32)
    o_ref[...] = (gate * _sigmoid(gate) * (o * lax.rsqrt(ms + RMS_EPS))).astype(o_ref.dtype)


def _gla(qk, v, g, lr, gate_w2_l, gate_b_l, batch, seq, tg):
    tpb = seq // tg
    row = lambda b, i: (b * tpb + i, 0)
    w2 = jnp.zeros((LANE, N_GROUP_HEADS * GLA_DK), F32).at[N_GATE_COLS:N_GATE_COLS + GLA_RANK].set(gate_w2_l)
    return pl.pallas_call(
        functools.partial(_gla_kernel, n_chunks=tg // GLA_CHUNK),
        out_shape=jax.ShapeDtypeStruct((batch * seq, GROUP_WIDTH), BF16),
        grid=(batch, tpb),
        in_specs=[pl.BlockSpec((tg, 2 * N_GROUP_HEADS * GLA_DK), row),
                  pl.BlockSpec((tg, GROUP_WIDTH), row),
                  pl.BlockSpec((tg, GROUP_WIDTH), row),
                  pl.BlockSpec((tg, LANE), row),
                  pl.BlockSpec((LANE, N_GROUP_HEADS * GLA_DK), lambda b, i: (0, 0)),
                  pl.BlockSpec((1, N_GROUP_HEADS * GLA_DK), lambda b, i: (0, 0))],
        out_specs=pl.BlockSpec((tg, GROUP_WIDTH), row),
        scratch_shapes=[pltpu.VMEM((N_GROUP_HEADS * GLA_DK, N_GROUP_HEADS * GLA_DV), F32)],
        compiler_params=_cparams(("parallel", "arbitrary")),
        name="gla",
    )(qk, v, g, lr, w2, gate_b_l.reshape(1, -1))


def _gmlp_kernel(uv_ref, w_ref, bias_ref, o_ref, *, n_chunks):
    T, G = GMLP_CHUNK, N_GROUP_HEADS
    gdim = GROUP_WIDTH // G
    gmean = _group_mean_matrix(GROUP_WIDTH, gdim)
    tri = (lax.broadcasted_iota(jnp.int32, (T, T), 0)
           >= lax.broadcasted_iota(jnp.int32, (T, T), 1))
    head = lax.broadcasted_iota(jnp.int32, (T, GROUP_WIDTH), 1) // gdim
    w = [jnp.where(tri, w_ref[g], 0.0).astype(BF16) for g in range(G)]
    v = uv_ref[:, GROUP_WIDTH:2 * GROUP_WIDTH]
    d = v.astype(F32) - jnp.dot(v, gmean, preferred_element_type=F32)
    vn = (d * lax.rsqrt(_dot_lhs3(d * d, gmean) + RMS_EPS)).astype(BF16)
    mixes = [[jnp.dot(w[g], vn[ci * T:(ci + 1) * T], preferred_element_type=F32) for g in range(G)]
             for ci in range(n_chunks)]
    for ci in range(n_chunks):
        rows = slice(ci * T, (ci + 1) * T)
        mixed = bias_ref[...]
        for g in range(G):
            mixed = mixed + jnp.where(head == g, mixes[ci][g], 0.0)
        o_ref[rows, :] = (uv_ref[rows, 0:GROUP_WIDTH].astype(F32) * mixed).astype(o_ref.dtype)


def _gmlp(uv, gmlp_ws_l, gmlp_b_l, rows, tc):
    gdim = GROUP_WIDTH // N_GROUP_HEADS
    bias = jnp.repeat(gmlp_b_l.T, gdim, axis=1)
    return pl.pallas_call(
        functools.partial(_gmlp_kernel, n_chunks=tc // GMLP_CHUNK),
        out_shape=jax.ShapeDtypeStruct((rows, GROUP_WIDTH), BF16),
        grid=(rows // tc,),
        in_specs=[pl.BlockSpec((tc, 2 * GROUP_WIDTH), lambda i: (i, 0)),
                  pl.BlockSpec((N_GROUP_HEADS, GMLP_CHUNK, GMLP_CHUNK), lambda i: (0, 0, 0)),
                  pl.BlockSpec((GMLP_CHUNK, GROUP_WIDTH), lambda i: (0, 0))],
        out_specs=pl.BlockSpec((tc, GROUP_WIDTH), lambda i: (i, 0)),
        compiler_params=_cparams(("parallel",)),
        name="gmlp",
    )(uv, gmlp_ws_l, bias)


def _tail_kernel(conv_ref, halo_ref, cw_ref, nsa_ref, gla_ref, gmlp_ref, gain_ref, wo_ref,
                 x_ref, mod_ref, g_ref, wu_ref, wd_ref, o_ref, *, tpb, tf):
    GW = GROUP_WIDTH
    tm = conv_ref.shape[0]
    nh = halo_ref.shape[0]
    z = conv_ref[:, 2 * GW:3 * GW].astype(F32) * conv_ref[:, 0:GW].astype(F32)
    zh = halo_ref[:, 2 * GW:3 * GW].astype(F32) * halo_ref[:, 0:GW].astype(F32)
    zh = jnp.where(pl.program_id(0) % tpb == 0, 0.0, zh)
    row = lax.broadcasted_iota(jnp.int32, (tm, GW), 0)
    y = cw_ref[CONV_WIDTH - 1:CONV_WIDTH, :] * z
    for back in range(1, CONV_WIDTH):
        shifted = pltpu.roll(z, back, axis=0)
        for r in range(back):
            shifted = jnp.where(row == r, zh[nh - back + r:nh - back + r + 1, :], shifted)
        y = y + cw_ref[CONV_WIDTH - 1 - back:CONV_WIDTH - back, :] * shifted
    groups = (conv_ref[:, GW:2 * GW].astype(F32) * y, nsa_ref[...], gla_ref[...], gmlp_ref[...])
    acc = jnp.zeros((tm, D_MODEL), F32)
    for k, o in enumerate(groups):
        n = (_rms(o.astype(F32)) * gain_ref[k:k + 1, :]).astype(BF16)
        acc = acc + jnp.dot(n, wo_ref[0, k * GW:(k + 1) * GW, :], preferred_element_type=F32)
    x = x_ref[...] + mod_ref[0, 2:3, :] * (_rms(acc) * g_ref[1:2, :])

    h = _rms(x) * g_ref[2:3, :]
    hb = (h * (1.0 + mod_ref[0, 4:5, :]) + mod_ref[0, 3:4, :]).astype(BF16)
    acc = jnp.zeros(x.shape, F32)
    for f in range(D_FF // tf):
        u = jnp.dot(hb, wu_ref[0, :, f * tf:(f + 1) * tf], preferred_element_type=F32)
        u = jnp.maximum(u, 0.0)
        acc = acc + jnp.dot((u * u).astype(BF16), wd_ref[0, f * tf:(f + 1) * tf, :],
                            preferred_element_type=F32)
    o_ref[...] = x + mod_ref[0, 5:6, :] * (_rms(acc) * g_ref[3:4, :])


def _layer_tail(conv, nsa, gla, gmlp, conv_w_l, grp_gain_l, wo_all, wu_all, wd_all, layer,
                x2, mod_l, norm_g_l, seq, tm):
    rows = x2.shape[0]
    tpb = seq // tm
    halo = 16
    hpt = tm // halo
    gw = lambda i: (i, 0)
    const = lambda i: (0, 0)
    weight = lambda shape: pl.BlockSpec((1,) + shape, lambda i: (layer, 0, 0),
                                        pipeline_mode=pl.Buffered(1))
    return pl.pallas_call(
        functools.partial(_tail_kernel, tpb=tpb, tf=D_MODEL),
        out_shape=jax.ShapeDtypeStruct((rows, D_MODEL), F32),
        grid=(rows // tm,),
        in_specs=[pl.BlockSpec((tm, 3 * GROUP_WIDTH), gw),
                  pl.BlockSpec((halo, 3 * GROUP_WIDTH), lambda i: (jnp.maximum(i * hpt - 1, 0), 0)),
                  pl.BlockSpec((CONV_WIDTH, GROUP_WIDTH), const),
                  pl.BlockSpec((tm, GROUP_WIDTH), gw),
                  pl.BlockSpec((tm, GROUP_WIDTH), gw),
                  pl.BlockSpec((tm, GROUP_WIDTH), gw),
                  pl.BlockSpec((N_MIXERS, GROUP_WIDTH), const),
                  weight((D_MODEL, D_MODEL)),
                  pl.BlockSpec((tm, D_MODEL), gw),
                  pl.BlockSpec((1, N_MOD, D_MODEL), lambda i: (i // tpb, 0, 0)),
                  pl.BlockSpec((4, D_MODEL), const),
                  weight((D_MODEL, D_FF)),
                  weight((D_FF, D_MODEL))],
        out_specs=pl.BlockSpec((tm, D_MODEL), gw),
        compiler_params=_cparams(("parallel",)),
        name="layer_tail",
    )(conv, conv, conv_w_l, nsa, gla, gmlp, grp_gain_l.reshape(N_MIXERS, GROUP_WIDTH), wo_all,
      x2, mod_l, norm_g_l, wu_all, wd_all)


def _pack_in_proj(w_in):
    offs = np.cumsum((0,) + (3 * GROUP_WIDTH, GROUP_WIDTH, 2 * N_BRANCH * HEAD_DIM, N_GATE_COLS,
                             2 * N_GROUP_HEADS * GLA_DK, GROUP_WIDTH, GROUP_WIDTH, GLA_RANK,
                             2 * GROUP_WIDTH))
    col = lambda a, b: w_in[..., offs[a]:offs[b]]
    narrow_pad = jnp.zeros(w_in.shape[:-1] + (LANE - N_GATE_COLS - GLA_RANK,), w_in.dtype)
    return jnp.concatenate([col(0, 3),
                            col(3, 4), col(7, 8), narrow_pad,
                            col(4, 7),
                            col(8, 9)], axis=-1).astype(BF16)


def kernel(x, c, positions, w_in, conv_w, cmp_pos, cmp_w1, cmp_w2, gla_gate_w2, gla_gate_b, gmlp_ws, gmlp_b, grp_gain, w_o, norm_g, w_mod, b_mod, w_up, w_down):
    batch, seq, _ = x.shape
    depth = w_in.shape[0]
    rows = batch * seq
    tm = min(512, seq)
    qs = min(128, seq)
    tq = min(2 * qs, seq)
    tk = min(256, seq)

    mod = _modulation(c, w_mod, b_mod).reshape(depth, batch, N_MOD, D_MODEL)
    cos_t, sin_t = _rope_tables(positions)
    x2 = x.reshape(rows, D_MODEL)
    w_in_b, w_o_b, w_up_b, w_down_b = _pack_in_proj(w_in), w_o.astype(BF16), w_up.astype(BF16), w_down.astype(BF16)
    for l in range(depth):
        conv, n_q, n_kv, narrow, l_qk, l_v, l_g, m_uv = _in_projection(
            x2, mod[l], norm_g[l], w_in_b, l, seq, tm)
        qp, qr, ks, vs, kw, vw = _nsa_prep(n_q, n_kv, cos_t, sin_t, batch, seq, tm)
        kc, vc = _nsa_compress(n_kv, cmp_pos[l], cmp_w1[l], cmp_w2[l], batch, seq)
        nsa = _nsa_attention(qp, qr, kc, vc, ks, vs, kw, vw, narrow, batch, seq, tq, qs, tk)
        gla = _gla(l_qk, l_v, l_g, narrow, gla_gate_w2[l], gla_gate_b[l], batch, seq, tm)
        gmlp = _gmlp(m_uv, gmlp_ws[l], gmlp_b[l], rows, tm)
        x2 = _layer_tail(conv, nsa, gla, gmlp, conv_w[l], grp_gain[l], w_o_b, w_up_b, w_down_b, l,
                         x2, mod[l], norm_g[l], seq, tm)
    return x2.reshape(batch, seq, D_MODEL)
```

```python
import functools

import numpy as np
import jax
import jax.numpy as jnp
from jax import lax
from jax.experimental import pallas as pl
from jax.experimental.pallas import tpu as pltpu

D_MODEL = 1024
N_MIXERS = 4
GROUP_WIDTH = D_MODEL // N_MIXERS
HEAD_DIM = 64
N_GROUP_HEADS = GROUP_WIDTH // HEAD_DIM
D_FF = 4 * D_MODEL
N_MOD = 6
RMS_EPS = 1e-6
NEG_INF = -1e30

ROPE_THETA = 500000.0
ROPE_DIM = HEAD_DIM // 4
ROPE_HALF = ROPE_DIM // 2

CONV_WIDTH = 3

CMP_BLOCK = 32
CMP_STRIDE = 16
CMP_HIDDEN = 2 * HEAD_DIM
SLC_BLOCK = 64
N_SLC = 16
N_INIT_BLOCKS = 1
N_LOCAL_BLOCKS = 2
FORCE_BONUS = 1e4
WINDOW = 512
N_BRANCH = 3
BLK_GROUP = 16
LOG2E = 1.4426950408889634
V_ROWS = HEAD_DIM + 16

GLA_DK = HEAD_DIM // 2
GLA_DV = HEAD_DIM
GLA_RANK = 16
GLA_TAU = 16.0
GLA_CHUNK = 64

GMLP_CHUNK = 128

LANE = 128
VMEM_LIMIT = 48 * 1024 * 1024

F32 = jnp.float32
BF16 = jnp.bfloat16

N_GATE_COLS = N_BRANCH * N_GROUP_HEADS
SEG_WIDTHS = (3 * GROUP_WIDTH,
              GROUP_WIDTH,
              2 * N_BRANCH * HEAD_DIM,
              LANE,
              2 * N_GROUP_HEADS * GLA_DK,
              GROUP_WIDTH,
              GROUP_WIDTH,
              2 * GROUP_WIDTH)
SEG_DTYPES = (BF16, BF16, F32, BF16, BF16, BF16, BF16, BF16)
SEG_OFFS = tuple(int(v) for v in np.cumsum((0,) + SEG_WIDTHS))
D_IN_PAD = SEG_OFFS[-1]
DOT_GROUPS = ((0,), (1,), (2, 3), (4,), (5,), (6,), (7,))


def _cparams(sem):
    return pltpu.CompilerParams(dimension_semantics=sem, vmem_limit_bytes=VMEM_LIMIT)


def _bdot(a, b):
    return jnp.dot(a.astype(BF16), b.astype(BF16), preferred_element_type=F32)


def _dot_nt(a, b):
    return lax.dot_general(a.astype(BF16), b.astype(BF16), (((1,), (1,)), ((), ())),
                           preferred_element_type=F32)


def _split2(a):
    hi = a.astype(BF16)
    lo = (a - hi.astype(F32)).astype(BF16)
    return hi, lo


def _split3(a):
    hi = a.astype(BF16)
    r = a - hi.astype(F32)
    mid = r.astype(BF16)
    lo = (r - mid.astype(F32)).astype(BF16)
    return hi, mid, lo


def _dot_lhs3(a, b_exact):
    hi, mid, lo = _split3(a)
    b = b_exact.astype(BF16)
    d = lambda p: jnp.dot(p, b, preferred_element_type=F32)
    return d(hi) + d(mid) + d(lo)


def _dot_rhs3(a_exact, b):
    hi, mid, lo = _split3(b)
    a = a_exact.astype(BF16)
    d = lambda p: jnp.dot(a, p, preferred_element_type=F32)
    return d(hi) + d(mid) + d(lo)


def _dot_f32(a, b):
    ah, al = _split2(a)
    bh, bl = _split2(b)
    d = lambda p, q: jnp.dot(p, q, preferred_element_type=F32)
    return d(ah, bh) + d(al, bh) + d(ah, bl)


def _rms(x):
    return x * lax.rsqrt(jnp.mean(x * x, axis=-1, keepdims=True) + RMS_EPS)


def _sigmoid(x):
    return 1.0 / (1.0 + jnp.exp(-x))


def _group_mean_matrix(width, group):
    r = lax.broadcasted_iota(jnp.int32, (width, width), 0) // group
    c = lax.broadcasted_iota(jnp.int32, (width, width), 1) // group
    return jnp.where(r == c, 1.0 / group, 0.0).astype(BF16)


def _mod_kernel(c_ref, w_ref, b_ref, o_ref):
    c = c_ref[...]
    cond = c * _sigmoid(c)
    o_ref[0] = _dot_f32(cond, w_ref[0]) + b_ref[0]


def _modulation(c, w_mod, b_mod):
    depth, d, n = w_mod.shape
    b = c.shape[0]
    tn = D_MODEL
    return pl.pallas_call(
        _mod_kernel,
        out_shape=jax.ShapeDtypeStruct((depth, b, n), F32),
        grid=(depth, n // tn),
        in_specs=[pl.BlockSpec((b, d), lambda l, j: (0, 0)),
                  pl.BlockSpec((1, d, tn), lambda l, j: (l, 0, j)),
                  pl.BlockSpec((1, 1, tn), lambda l, j: (l, 0, j))],
        out_specs=pl.BlockSpec((1, b, tn), lambda l, j: (l, 0, j)),
        compiler_params=_cparams(("parallel", "parallel")),
        name="adaln_mod",
    )(c, w_mod, b_mod.reshape(depth, 1, n))


def _rope_table_kernel(pos_ref, freq_ref, sign_ref, cos_ref, sin_ref):
    ang = pos_ref[...].astype(F32) * freq_ref[...]
    cos_ref[...] = jnp.cos(ang)
    sin_ref[...] = jnp.sin(ang) * sign_ref[...]


def _rope_tables(positions):
    b, s = positions.shape
    rows = b * s
    tr = min(1024, rows)
    inv_freq = ROPE_THETA ** (-jnp.arange(0, ROPE_DIM, 2, dtype=F32) / ROPE_DIM)
    lane = np.arange(LANE) % HEAD_DIM
    freq = jnp.where(lane < ROPE_DIM, inv_freq[lane % ROPE_HALF], 0.0).astype(F32)
    sign = jnp.asarray(np.where(lane < ROPE_HALF, -1.0, 1.0), dtype=F32)
    shape = jax.ShapeDtypeStruct((rows, LANE), F32)
    return pl.pallas_call(
        _rope_table_kernel,
        out_shape=(shape, shape),
        grid=(rows // tr,),
        in_specs=[pl.BlockSpec((tr, 1), lambda i: (i, 0)),
                  pl.BlockSpec((1, LANE), lambda i: (0, 0)),
                  pl.BlockSpec((1, LANE), lambda i: (0, 0))],
        out_specs=(pl.BlockSpec((tr, LANE), lambda i: (i, 0)),
                   pl.BlockSpec((tr, LANE), lambda i: (i, 0))),
        compiler_params=_cparams(("parallel",)),
        name="rope_tables",
    )(positions.reshape(rows, 1), freq.reshape(1, -1), sign.reshape(1, -1))


def _inproj_kernel(x_ref, mod_ref, g_ref, w_ref, *out_refs):
    h = _rms(x_ref[...]) * g_ref[0:1, :]
    h = h * (1.0 + mod_ref[0, 1:2, :]) + mod_ref[0, 0:1, :]
    hb = h.astype(BF16)
    for group in DOT_GROUPS:
        lo, hi = SEG_OFFS[group[0]], SEG_OFFS[group[-1] + 1]
        z = jnp.dot(hb, w_ref[0, :, lo:hi], preferred_element_type=F32)
        for k in group:
            o_ref = out_refs[k]
            o_ref[...] = z[:, SEG_OFFS[k] - lo:SEG_OFFS[k + 1] - lo].astype(o_ref.dtype)


def _in_projection(x2, mod_l, norm_g_l, w_packed, layer, seq, tm):
    rows = x2.shape[0]
    tpb = seq // tm
    outs = tuple(jax.ShapeDtypeStruct((rows, w), dt) for w, dt in zip(SEG_WIDTHS, SEG_DTYPES))
    return pl.pallas_call(
        _inproj_kernel,
        out_shape=outs,
        grid=(rows // tm,),
        in_specs=[pl.BlockSpec((tm, D_MODEL), lambda i: (i, 0)),
                  pl.BlockSpec((1, N_MOD, D_MODEL), lambda i: (i // tpb, 0, 0)),
                  pl.BlockSpec((4, D_MODEL), lambda i: (0, 0)),
                  pl.BlockSpec((1, D_MODEL, D_IN_PAD), lambda i: (layer, 0, 0))],
        out_specs=tuple(pl.BlockSpec((tm, w), lambda i: (i, 0)) for w in SEG_WIDTHS),
        compiler_params=_cparams(("parallel",)),
        name="in_proj",
    )(x2, mod_l, norm_g_l, w_packed)


def _rope(x, cos, sin):
    w = x.shape[-1]
    lane = lax.broadcasted_iota(jnp.int32, x.shape, 1) % HEAD_DIM
    swapped = jnp.where(lane < ROPE_HALF,
                        pltpu.roll(x, w - ROPE_HALF, axis=1),
                        pltpu.roll(x, ROPE_HALF, axis=1))
    return x * cos + swapped * sin


def _nsa_prep_kernel(q_ref, kv_ref, cos_ref, sin_ref,
                     qp_ref, qr_ref, ks_ref, vs_ref, kw_ref, vw_ref):
    tp = cos_ref.shape[0]
    cos = cos_ref[...]
    sin = sin_ref[...]
    q = q_ref[...].astype(F32) * (HEAD_DIM ** -0.5 * LOG2E)
    qp_ref[0] = q.T.astype(BF16)
    reps = GROUP_WIDTH // LANE
    qr_ref[0] = _rope(q, jnp.concatenate([cos] * reps, axis=1),
                      jnp.concatenate([sin] * reps, axis=1)).T.astype(BF16)
    lane = lax.broadcasted_iota(jnp.int32, (tp, LANE), 1)
    is_k = lane < HEAD_DIM
    cos_kv = jnp.where(is_k, cos, 1.0)
    sin_kv = jnp.where(is_k, sin, 0.0)
    slc = _rope(kv_ref[:, LANE:2 * LANE], cos_kv, sin_kv)
    win = _rope(kv_ref[:, 2 * LANE:3 * LANE], cos_kv, sin_kv)
    blk = ((pl.program_id(1) * tp + lax.broadcasted_iota(jnp.int32, (tp, LANE), 0)) // SLC_BLOCK) % BLK_GROUP
    onehot = jnp.where(lane - HEAD_DIM == blk, 1.0, 0.0)
    ks_ref[0] = jnp.where(is_k, slc, onehot).astype(BF16)
    ones_rows = jnp.where(lax.broadcasted_iota(jnp.int32, (V_ROWS - HEAD_DIM, tp), 0) == 0, 1.0, 0.0)
    vs_ref[0] = jnp.concatenate([slc.T[HEAD_DIM:, :], ones_rows], axis=0).astype(BF16)
    kw_ref[0] = win[:, :HEAD_DIM].astype(BF16)
    vw_ref[0] = jnp.concatenate([win.T[HEAD_DIM:, :], ones_rows], axis=0).astype(BF16)


def _nsa_prep(q, kv, cos_t, sin_t, batch, seq, tp):
    tpb = seq // tp
    row = lambda b, i: (b * tpb + i, 0)
    qs = jax.ShapeDtypeStruct((batch, GROUP_WIDTH, seq), BF16)
    ks = jax.ShapeDtypeStruct((batch, seq, LANE), BF16)
    kw = jax.ShapeDtypeStruct((batch, seq, HEAD_DIM), BF16)
    vs = jax.ShapeDtypeStruct((batch, V_ROWS, seq), BF16)
    q_spec = pl.BlockSpec((1, GROUP_WIDTH, tp), lambda b, i: (b, 0, i))
    ks_spec = pl.BlockSpec((1, tp, LANE), lambda b, i: (b, i, 0))
    kw_spec = pl.BlockSpec((1, tp, HEAD_DIM), lambda b, i: (b, i, 0))
    v_spec = pl.BlockSpec((1, V_ROWS, tp), lambda b, i: (b, 0, i))
    return pl.pallas_call(
        _nsa_prep_kernel,
        out_shape=(qs, qs, ks, vs, kw, vs),
        grid=(batch, tpb),
        in_specs=[pl.BlockSpec((tp, GROUP_WIDTH), row),
                  pl.BlockSpec((tp, 2 * N_BRANCH * HEAD_DIM), row),
                  pl.BlockSpec((tp, LANE), row),
                  pl.BlockSpec((tp, LANE), row)],
        out_specs=(q_spec, q_spec, ks_spec, v_spec, kw_spec, v_spec),
        compiler_params=_cparams(("parallel", "parallel")),
        name="nsa_prep",
    )(q, kv, cos_t, sin_t)


def _nsa_compress_kernel(kv_ref, pos_ref, w1_ref, w2_ref, kc_ref, vc_ref, *, n_half):
    half = CMP_BLOCK // 2
    assert half == CMP_STRIDE
    chunks = [kv_ref[pl.ds(r, n_half, stride=CMP_STRIDE), :] for r in range(half)]
    outs = []
    for j in range(2):
        first = jnp.zeros((n_half, CMP_HIDDEN), F32)
        second = jnp.zeros((n_half, CMP_HIDDEN), F32)
        for r in range(half):
            t = chunks[r][:, j * HEAD_DIM:(j + 1) * HEAD_DIM]
            first += _bdot(t + pos_ref[j, r:r + 1, :],
                           w1_ref[j, r * HEAD_DIM:(r + 1) * HEAD_DIM, :])
            second += _bdot(t + pos_ref[j, half + r:half + r + 1, :],
                            w1_ref[j, (half + r) * HEAD_DIM:(half + r + 1) * HEAD_DIM, :])
        pre = first + pltpu.roll(second, n_half - 1, axis=0)
        row = lax.broadcasted_iota(jnp.int32, pre.shape, 0)
        pre = jnp.where(row < n_half - 1, pre, 0.0)
        hid = pre * _sigmoid(pre)
        outs.append(_bdot(hid, w2_ref[j]))
    kc_ref[0] = outs[0].astype(BF16)
    vc_ref[0] = jnp.concatenate(outs, axis=1).T[HEAD_DIM:, :].astype(BF16)


def _nsa_compress(kv, cmp_pos_l, cmp_w1_l, cmp_w2_l, batch, seq):
    n_half = seq // CMP_STRIDE
    return pl.pallas_call(
        functools.partial(_nsa_compress_kernel, n_half=n_half),
        out_shape=(jax.ShapeDtypeStruct((batch, n_half, HEAD_DIM), BF16),
                   jax.ShapeDtypeStruct((batch, HEAD_DIM, n_half), BF16)),
        grid=(batch,),
        in_specs=[pl.BlockSpec((seq, LANE), lambda b: (b, 0)),
                  pl.BlockSpec((2, CMP_BLOCK, HEAD_DIM), lambda b: (0, 0, 0)),
                  pl.BlockSpec((2, CMP_BLOCK * HEAD_DIM, CMP_HIDDEN), lambda b: (0, 0, 0)),
                  pl.BlockSpec((2, CMP_HIDDEN, HEAD_DIM), lambda b: (0, 0, 0))],
        out_specs=(pl.BlockSpec((1, n_half, HEAD_DIM), lambda b: (b, 0, 0)),
                   pl.BlockSpec((1, HEAD_DIM, n_half), lambda b: (b, 0, 0))),
        compiler_params=_cparams(("parallel",)),
        name="nsa_compress",
    )(kv, cmp_pos_l, cmp_w1_l, cmp_w2_l)


def _nsa_attn_kernel(qp_ref, qr_ref, kc_ref, vc_ref, ks_ref, vs_ref, kw_ref, vw_ref, gate_ref,
                     ovl_ref, o_ref, sel_ref, s_ref,
                     *, tq, qs, tk, seq, n_top, win_len):
    H = N_GROUP_HEADS
    W = H * qs
    G = tq // qs
    n_cmp_pad = kc_ref.shape[1]
    n_cmp = n_cmp_pad - 1
    n_blk = seq // SLC_BLOCK
    blk_per_tile = tk // SLC_BLOCK
    t_step = pl.program_id(1) * tq

    def heads_on_lanes(ref, g):
        return jnp.concatenate([ref[0, h * HEAD_DIM:(h + 1) * HEAD_DIM, g * qs:(g + 1) * qs]
                                for h in range(H)], axis=1)

    def tile_heads(a):
        return jnp.concatenate([a] * H, axis=1)

    t_sub = [t_step + g * qs for g in range(G)]
    q_rot = [heads_on_lanes(qr_ref, g) for g in range(G)]
    w_start = [pl.multiple_of(jnp.maximum(t0 + qs - win_len, 0), qs) for t0 in t_sub]
    s_cmp = [jnp.dot(kc_ref[0], heads_on_lanes(qp_ref, g), preferred_element_type=F32) for g in range(G)]
    s_win = [jnp.dot(kw_ref[0, pl.ds(w0, win_len), :], q, preferred_element_type=F32)
             for w0, q in zip(w_start, q_rot)]

    p_cmp = []
    for t0, s in zip(t_sub, s_cmp):
        n_c = lax.broadcasted_iota(jnp.int32, (n_cmp_pad, qs), 0)
        t_c = t0 + lax.broadcasted_iota(jnp.int32, (n_cmp_pad, qs), 1)
        bias_c = jnp.where((n_c * CMP_STRIDE + CMP_BLOCK - 1 <= t_c) & (n_c < n_cmp), 0.0, NEG_INF)
        s_c = s + tile_heads(bias_c)
        e_c = jnp.exp2(s_c - jnp.max(s_c, axis=0, keepdims=True))
        any_valid = jnp.where(t_c[0:1, :] >= CMP_BLOCK - 1, 1.0, 0.0)
        p_cmp.append(e_c * (tile_heads(any_valid) / jnp.sum(e_c, axis=0, keepdims=True)))
    o_cmp = [jnp.dot(vc_ref[0], p_c.astype(BF16), preferred_element_type=F32) for p_c in p_cmp]

    p_slc = []
    for p_c in p_cmp:
        p_sum = p_c[:, 0:qs]
        for h in range(1, H):
            p_sum = p_sum + p_c[:, h * qs:(h + 1) * qs]
        ps_hi, ps_lo = _split2(p_sum)
        p_slc.append(jnp.dot(ovl_ref[...], ps_hi, preferred_element_type=F32)
                     + jnp.dot(ovl_ref[...], ps_lo, preferred_element_type=F32))

    o_win = []
    for t0, w0, s in zip(t_sub, w_start, s_win):
        diff = (t0 + lax.broadcasted_iota(jnp.int32, (win_len, qs), 1)
                - w0 - lax.broadcasted_iota(jnp.int32, (win_len, qs), 0))
        s_w = s + tile_heads(jnp.where((diff >= 0) & (diff < WINDOW), 0.0, NEG_INF))
        e_w = jnp.exp2((s_w - jnp.max(s_w, axis=0, keepdims=True)).astype(BF16))
        o_w = jnp.dot(vw_ref[0, :, pl.ds(w0, win_len)], e_w, preferred_element_type=F32)
        o_win.append(o_w[:HEAD_DIM] / o_w[HEAD_DIM:HEAD_DIM + 1])

    for g, (t0, imp) in enumerate(zip(t_sub, p_slc)):
        j_s = lax.broadcasted_iota(jnp.int32, (n_blk, qs), 0)
        cur = (t0 + lax.broadcasted_iota(jnp.int32, (n_blk, qs), 1)) // SLC_BLOCK
        forced = (j_s < N_INIT_BLOCKS) | ((cur - j_s >= 0) & (cur - j_s < N_LOCAL_BLOCKS))
        score = jnp.where(j_s <= cur, imp + jnp.where(forced, FORCE_BONUS, 0.0), NEG_INF)
        rank = jnp.zeros((n_blk, qs), F32)
        for jp in range(n_blk):
            row = score[jp:jp + 1, :]
            rank = rank + jnp.where(j_s > jp, jnp.where(row >= score, 1.0, 0.0),
                                    jnp.where(row > score, 1.0, 0.0))
        sel_ref[g] = jnp.where(rank < n_top, 1.0, 0.0)

    pad_rows = jnp.zeros((LANE - HEAD_DIM - BLK_GROUP, W), BF16)

    def augmented_queries(g, group):
        rows = sel_ref[g, pl.ds(pl.multiple_of(group * BLK_GROUP, BLK_GROUP), BLK_GROUP), :]
        bias = tile_heads((rows - 1.0) * -NEG_INF).astype(BF16)
        return jnp.concatenate([q_rot[g], bias, pad_rows], axis=0)

    t_blk0 = t_step // SLC_BLOCK
    strip = 2 * qs
    n_strip = W // strip
    init = []
    t_al = pl.multiple_of(t_step, tq)
    s_diag = [jnp.dot(ks_ref[0, pl.ds(t_al, (g + 1) * qs), :], augmented_queries(g, t_blk0 // BLK_GROUP),
                      preferred_element_type=F32) for g in range(G)]
    for g in range(G):
        n_diag = (g + 1) * qs
        causal = (lax.broadcasted_iota(jnp.int32, (n_diag, qs), 0)
                  <= g * qs + lax.broadcasted_iota(jnp.int32, (n_diag, qs), 1))
        s = s_diag[g] + tile_heads(jnp.where(causal, 0.0, NEG_INF))
        m_new = jnp.max(s, axis=0, keepdims=True)
        p = jnp.exp2((s - m_new).astype(BF16))
        acc = jnp.dot(vs_ref[0, :, pl.ds(t_al, n_diag)], p, preferred_element_type=F32)
        for c in range(n_strip):
            cols = slice(c * strip, (c + 1) * strip)
            init.append((m_new[:, cols], acc[:, cols]))

    th = tk // 2

    def score_strips(kt, half):
        k0 = pl.multiple_of(kt * tk + half * th, th)
        keys = ks_ref[0, pl.ds(k0, th), :]
        for g in range(G):
            q_aug = augmented_queries(g, kt // (BLK_GROUP // blk_per_tile))
            for c in range(n_strip):
                s_ref[half, g * n_strip + c] = jnp.dot(keys, q_aug[:, c * strip:(c + 1) * strip],
                                                       preferred_element_type=F32)

    def softmax_strips(kt, half, stats):
        k0 = pl.multiple_of(kt * tk + half * th, th)
        vals = vs_ref[0, :, pl.ds(k0, th)]
        out = []
        for c, (m_old, acc_old) in enumerate(stats):
            s = s_ref[half, c]
            m_new = jnp.maximum(m_old, jnp.max(s, axis=0, keepdims=True))
            p = jnp.exp2((s - m_new).astype(BF16))
            alpha = jnp.exp2(m_old - m_new)
            out.append((m_new, alpha * acc_old + jnp.dot(vals, p, preferred_element_type=F32)))
        return tuple(out)

    n_tiles = t_step // tk

    def key_tile(kt, stats):
        score_strips(kt, 1)
        stats = softmax_strips(kt, 0, stats)
        score_strips(jnp.minimum(kt + 1, n_tiles - 1), 0)
        return softmax_strips(kt, 1, stats)

    score_strips(0, 0)
    final = lax.fori_loop(0, n_tiles, key_tile, tuple(init))

    for g in range(G):
        gate = _sigmoid(gate_ref[g * qs:(g + 1) * qs, :].astype(F32).T)
        o_sel = jnp.concatenate([acc_fin[:HEAD_DIM] / acc_fin[HEAD_DIM:HEAD_DIM + 1]
                                 for _, acc_fin in final[g * n_strip:(g + 1) * n_strip]], axis=1)
        outs = []
        for h in range(H):
            c = h * N_BRANCH
            cols = slice(h * qs, (h + 1) * qs)
            outs.append(gate[c:c + 1, :] * o_cmp[g][:, cols] + gate[c + 1:c + 2, :] * o_sel[:, cols]
                        + gate[c + 2:c + 3, :] * o_win[g][:, cols])
        o_ref[g * qs:(g + 1) * qs, :] = jnp.concatenate(outs, axis=0).T.astype(o_ref.dtype)


def _cmp_to_slc_t(n_cmp_pad, n_blk):
    n_cmp = n_cmp_pad - 1
    cs = np.arange(n_cmp) * CMP_STRIDE
    ss = np.arange(n_blk) * SLC_BLOCK
    ov = np.minimum(cs[:, None] + CMP_BLOCK, ss[None, :] + SLC_BLOCK) - np.maximum(cs[:, None], ss[None, :])
    m = np.zeros((n_cmp_pad, n_blk), np.float32)
    m[:n_cmp] = np.clip(ov, 0, None) / CMP_STRIDE
    return jnp.asarray(m.T, dtype=BF16)


def _nsa_attention(qp, qr, kc, vc, ks, vs, kw, vw, gates, batch, seq, tq, qs, tk):
    n_blk = seq // SLC_BLOCK
    n_cmp_pad = kc.shape[1]
    n_top = min(N_SLC, n_blk)
    win_len = min(WINDOW + qs, seq)
    tpb = seq // tq
    G = tq // qs
    W = N_GROUP_HEADS * qs
    full = lambda shape: pl.BlockSpec((1,) + shape, lambda b, i: (b, 0, 0))
    q_spec = pl.BlockSpec((1, GROUP_WIDTH, tq), lambda b, i: (b, 0, i))
    return pl.pallas_call(
        functools.partial(_nsa_attn_kernel, tq=tq, qs=qs, tk=tk, seq=seq, n_top=n_top, win_len=win_len),
        out_shape=jax.ShapeDtypeStruct((batch * seq, GROUP_WIDTH), BF16),
        grid=(batch, tpb),
        in_specs=[q_spec, q_spec,
                  full((n_cmp_pad, HEAD_DIM)), full((HEAD_DIM, n_cmp_pad)),
                  full((seq, LANE)), full((V_ROWS, seq)),
                  full((seq, HEAD_DIM)), full((V_ROWS, seq)),
                  pl.BlockSpec((tq, LANE), lambda b, i: (b * tpb + i, 0)),
                  pl.BlockSpec((n_blk, n_cmp_pad), lambda b, i: (0, 0))],
        out_specs=pl.BlockSpec((tq, GROUP_WIDTH), lambda b, i: (b * tpb + i, 0)),
        scratch_shapes=[pltpu.VMEM((G, n_blk, qs), F32),
                        pltpu.VMEM((2, W * G // (2 * qs), tk // 2, 2 * qs), F32)],
        compiler_params=_cparams(("parallel", "parallel")),
        name="nsa_attn",
    )(qp, qr, kc, vc, ks, vs, kw, vw, gates, _cmp_to_slc_t(n_cmp_pad, n_blk))


def _gla_kernel(qk_ref, v_ref, g_ref, lr_ref, w2_ref, gb_ref, o_ref, state_ref, *, n_chunks):
    H, C = N_GROUP_HEADS, GLA_CHUNK
    DKW = H * GLA_DK
    DVW = H * GLA_DV

    @pl.when(pl.program_id(1) == 0)
    def _():
        state_ref[...] = jnp.zeros(state_ref.shape, F32)

    tri = (lax.broadcasted_iota(jnp.int32, (C, C), 0)
           >= lax.broadcasted_iota(jnp.int32, (C, C), 1))
    tri_b = jnp.where(tri, 1.0, 0.0).astype(BF16)
    qhead = lax.broadcasted_iota(jnp.int32, (C, DKW), 1) // GLA_DK
    vhead = lax.broadcasted_iota(jnp.int32, (C, DVW), 1) // GLA_DV
    causal = jnp.concatenate([tri] * H, axis=0)
    state_mask = (lax.broadcasted_iota(jnp.int32, (DKW, DVW), 0) // GLA_DK
                  == lax.broadcasted_iota(jnp.int32, (DKW, DVW), 1) // GLA_DV)

    pre = _dot_rhs3(lr_ref[...], w2_ref[...]) + gb_ref[...]
    log_a = (jnp.minimum(pre, 0.0) - jnp.log1p(jnp.exp(-jnp.abs(pre)))) / GLA_TAU
    q_all = qk_ref[:, 0:DKW].astype(F32) * (GLA_DK ** -0.5)
    k_all = qk_ref[:, DKW:2 * DKW].astype(F32)

    chunk_rows = [slice(ci * C, (ci + 1) * C) for ci in range(n_chunks)]
    bf = lambda a, w: jnp.dot(a, w, preferred_element_type=F32)
    la_hi, la_mid, la_lo = _split3(log_a)
    b_all = [bf(tri_b, la_hi[r]) + bf(tri_b, la_mid[r]) + bf(tri_b, la_lo[r]) for r in chunk_rows]
    v_all = [v_ref[r, :] for r in chunk_rows]
    q_stack, k_in, q_dec, k_out_t, decay = [], [], [], [], []
    for r, b in zip(chunk_rows, b_all):
        b_mid = b[C // 2:C // 2 + 1, :]
        b_last = b[C - 1:C, :]
        q, k = q_all[r], k_all[r]
        q_in = q * jnp.exp(b - b_mid)
        q_stack.append(jnp.concatenate([jnp.where(qhead == h, q_in, 0.0) for h in range(H)],
                                       axis=0).astype(BF16))
        k_in.append((k * jnp.exp(b_mid - b)).astype(BF16))
        q_dec.append((q * jnp.exp(b)).astype(BF16))
        k_out_t.append((k * jnp.exp(b_last - b)).T.astype(BF16))
        decay.append(jnp.exp(jnp.broadcast_to(b_last, (8, DKW))).T[:, 0:1])
    att = [lax.dot_general(qs_, ki_, (((1,), (1,)), ((), ())), preferred_element_type=F32)
           for qs_, ki_ in zip(q_stack, k_in)]
    att = [jnp.where(causal, a, 0.0).astype(BF16) for a in att]
    o_stack = [bf(a, v) for a, v in zip(att, v_all)]
    kv_new = [jnp.where(state_mask, bf(kt_, v), 0.0) for kt_, v in zip(k_out_t, v_all)]
    o_intra = []
    for os_ in o_stack:
        o = jnp.where(vhead == 0, os_[0:C], 0.0)
        for h in range(1, H):
            o = o + jnp.where(vhead == h, os_[h * C:(h + 1) * C], 0.0)
        o_intra.append(o)

    state = state_ref[...]
    o_chunks = []
    for ci in range(n_chunks):
        o_chunks.append(o_intra[ci] + bf(q_dec[ci], state.astype(BF16)))
        state = decay[ci] * state + kv_new[ci]
    state_ref[...] = state

    o = jnp.concatenate(o_chunks, axis=0)
    ms = _dot_lhs3(o * o, _group_mean_matrix(DVW, GLA_DV))
    gate = g_ref[...].astype(F32)
    o_ref[...] = (gate * _sigmoid(gate) * (o * lax.rsqrt(ms + RMS_EPS))).astype(o_ref.dtype)


def _gla(qk, v, g, lr, gate_w2_l, gate_b_l, batch, seq, tg):
    tpb = seq // tg
    row = lambda b, i: (b * tpb + i, 0)
    w2 = jnp.zeros((LANE, N_GROUP_HEADS * GLA_DK), F32).at[N_GATE_COLS:N_GATE_COLS + GLA_RANK].set(gate_w2_l)
    return pl.pallas_call(
        functools.partial(_gla_kernel, n_chunks=tg // GLA_CHUNK),
        out_shape=jax.ShapeDtypeStruct((batch * seq, GROUP_WIDTH), BF16),
        grid=(batch, tpb),
        in_specs=[pl.BlockSpec((tg, 2 * N_GROUP_HEADS * GLA_DK), row),
                  pl.BlockSpec((tg, GROUP_WIDTH), row),
                  pl.BlockSpec((tg, GROUP_WIDTH), row),
                  pl.BlockSpec((tg, LANE), row),
                  pl.BlockSpec((LANE, N_GROUP_HEADS * GLA_DK), lambda b, i: (0, 0)),
                  pl.BlockSpec((1, N_GROUP_HEADS * GLA_DK), lambda b, i: (0, 0))],
        out_specs=pl.BlockSpec((tg, GROUP_WIDTH), row),
        scratch_shapes=[pltpu.VMEM((N_GROUP_HEADS * GLA_DK, N_GROUP_HEADS * GLA_DV), F32)],
        compiler_params=_cparams(("parallel", "arbitrary")),
        name="gla",
    )(qk, v, g, lr, w2, gate_b_l.reshape(1, -1))


def _gmlp_kernel(uv_ref, w_ref, bias_ref, o_ref, *, n_chunks):
    T, G = GMLP_CHUNK, N_GROUP_HEADS
    gdim = GROUP_WIDTH // G
    gmean = _group_mean_matrix(GROUP_WIDTH, gdim)
    tri = (lax.broadcasted_iota(jnp.int32, (T, T), 0)
           >= lax.broadcasted_iota(jnp.int32, (T, T), 1))
    head = lax.broadcasted_iota(jnp.int32, (T, GROUP_WIDTH), 1) // gdim
    w = [jnp.where(tri, w_ref[g], 0.0).astype(BF16) for g in range(G)]
    v = uv_ref[:, GROUP_WIDTH:2 * GROUP_WIDTH]
    d = v.astype(F32) - jnp.dot(v, gmean, preferred_element_type=F32)
    vn = (d * lax.rsqrt(_dot_lhs3(d * d, gmean) + RMS_EPS)).astype(BF16)
    mixes = [[jnp.dot(w[g], vn[ci * T:(ci + 1) * T], preferred_element_type=F32) for g in range(G)]
             for ci in range(n_chunks)]
    for ci in range(n_chunks):
        rows = slice(ci * T, (ci + 1) * T)
        mixed = bias_ref[...]
        for g in range(G):
            mixed = mixed + jnp.where(head == g, mixes[ci][g], 0.0)
        o_ref[rows, :] = (uv_ref[rows, 0:GROUP_WIDTH].astype(F32) * mixed).astype(o_ref.dtype)


def _gmlp(uv, gmlp_ws_l, gmlp_b_l, rows, tc):
    gdim = GROUP_WIDTH // N_GROUP_HEADS
    bias = jnp.repeat(gmlp_b_l.T, gdim, axis=1)
    return pl.pallas_call(
        functools.partial(_gmlp_kernel, n_chunks=tc // GMLP_CHUNK),
        out_shape=jax.ShapeDtypeStruct((rows, GROUP_WIDTH), BF16),
        grid=(rows // tc,),
        in_specs=[pl.BlockSpec((tc, 2 * GROUP_WIDTH), lambda i: (i, 0)),
                  pl.BlockSpec((N_GROUP_HEADS, GMLP_CHUNK, GMLP_CHUNK), lambda i: (0, 0, 0)),
                  pl.BlockSpec((GMLP_CHUNK, GROUP_WIDTH), lambda i: (0, 0))],
        out_specs=pl.BlockSpec((tc, GROUP_WIDTH), lambda i: (i, 0)),
        compiler_params=_cparams(("parallel",)),
        name="gmlp",
    )(uv, gmlp_ws_l, bias)


def _tail_kernel(conv_ref, halo_ref, cw_ref, nsa_ref, gla_ref, gmlp_ref, gain_ref, wo_ref,
                 x_ref, mod_ref, g_ref, wu_ref, wd_ref, o_ref, *, tpb, tf):
    GW = GROUP_WIDTH
    tm = conv_ref.shape[0]
    nh = halo_ref.shape[0]
    z = conv_ref[:, 2 * GW:3 * GW].astype(F32) * conv_ref[:, 0:GW].astype(F32)
    zh = halo_ref[:, 2 * GW:3 * GW].astype(F32) * halo_ref[:, 0:GW].astype(F32)
    zh = jnp.where(pl.program_id(0) % tpb == 0, 0.0, zh)
    row = lax.broadcasted_iota(jnp.int32, (tm, GW), 0)
    y = cw_ref[CONV_WIDTH - 1:CONV_WIDTH, :] * z
    for back in range(1, CONV_WIDTH):
        shifted = pltpu.roll(z, back, axis=0)
        for r in range(back):
            shifted = jnp.where(row == r, zh[nh - back + r:nh - back + r + 1, :], shifted)
        y = y + cw_ref[CONV_WIDTH - 1 - back:CONV_WIDTH - back, :] * shifted
    groups = (conv_ref[:, GW:2 * GW].astype(F32) * y, nsa_ref[...], gla_ref[...], gmlp_ref[...])
    acc = jnp.zeros((tm, D_MODEL), F32)
    for k, o in enumerate(groups):
        n = (_rms(o.astype(F32)) * gain_ref[k:k + 1, :]).astype(BF16)
        acc = acc + jnp.dot(n, wo_ref[0, k * GW:(k + 1) * GW, :], preferred_element_type=F32)
    x = x_ref[...] + mod_ref[0, 2:3, :] * (_rms(acc) * g_ref[1:2, :])

    h = _rms(x) * g_ref[2:3, :]
    hb = (h * (1.0 + mod_ref[0, 4:5, :]) + mod_ref[0, 3:4, :]).astype(BF16)
    acc = jnp.zeros(x.shape, F32)
    for f in range(D_FF // tf):
        u = jnp.dot(hb, wu_ref[0, :, f * tf:(f + 1) * tf], preferred_element_type=F32)
        u = jnp.maximum(u, 0.0)
        acc = acc + jnp.dot((u * u).astype(BF16), wd_ref[0, f * tf:(f + 1) * tf, :],
                            preferred_element_type=F32)
    o_ref[...] = x + mod_ref[0, 5:6, :] * (_rms(acc) * g_ref[3:4, :])


def _layer_tail(conv, nsa, gla, gmlp, conv_w_l, grp_gain_l, wo_all, wu_all, wd_all, layer,
                x2, mod_l, norm_g_l, seq, tm):
    rows = x2.shape[0]
    tpb = seq // tm
    halo = 16
    hpt = tm // halo
    gw = lambda i: (i, 0)
    const = lambda i: (0, 0)
    weight = lambda shape: pl.BlockSpec((1,) + shape, lambda i: (layer, 0, 0),
                                        pipeline_mode=pl.Buffered(1))
    return pl.pallas_call(
        functools.partial(_tail_kernel, tpb=tpb, tf=D_MODEL),
        out_shape=jax.ShapeDtypeStruct((rows, D_MODEL), F32),
        grid=(rows // tm,),
        in_specs=[pl.BlockSpec((tm, 3 * GROUP_WIDTH), gw),
                  pl.BlockSpec((halo, 3 * GROUP_WIDTH), lambda i: (jnp.maximum(i * hpt - 1, 0), 0)),
                  pl.BlockSpec((CONV_WIDTH, GROUP_WIDTH), const),
                  pl.BlockSpec((tm, GROUP_WIDTH), gw),
                  pl.BlockSpec((tm, GROUP_WIDTH), gw),
                  pl.BlockSpec((tm, GROUP_WIDTH), gw),
                  pl.BlockSpec((N_MIXERS, GROUP_WIDTH), const),
                  weight((D_MODEL, D_MODEL)),
                  pl.BlockSpec((tm, D_MODEL), gw),
                  pl.BlockSpec((1, N_MOD, D_MODEL), lambda i: (i // tpb, 0, 0)),
                  pl.BlockSpec((4, D_MODEL), const),
                  weight((D_MODEL, D_FF)),
                  weight((D_FF, D_MODEL))],
        out_specs=pl.BlockSpec((tm, D_MODEL), gw),
        compiler_params=_cparams(("parallel",)),
        name="layer_tail",
    )(conv, conv, conv_w_l, nsa, gla, gmlp, grp_gain_l.reshape(N_MIXERS, GROUP_WIDTH), wo_all,
      x2, mod_l, norm_g_l, wu_all, wd_all)


def _pack_in_proj(w_in):
    offs = np.cumsum((0,) + (3 * GROUP_WIDTH, GROUP_WIDTH, 2 * N_BRANCH * HEAD_DIM, N_GATE_COLS,
                             2 * N_GROUP_HEADS * GLA_DK, GROUP_WIDTH, GROUP_WIDTH, GLA_RANK,
                             2 * GROUP_WIDTH))
    col = lambda a, b: w_in[..., offs[a]:offs[b]]
    narrow_pad = jnp.zeros(w_in.shape[:-1] + (LANE - N_GATE_COLS - GLA_RANK,), w_in.dtype)
    return jnp.concatenate([col(0, 3),
                            col(3, 4), col(7, 8), narrow_pad,
                            col(4, 7),
                            col(8, 9)], axis=-1).astype(BF16)


def kernel(x, c, positions, w_in, conv_w, cmp_pos, cmp_w1, cmp_w2, gla_gate_w2, gla_gate_b, gmlp_ws, gmlp_b, grp_gain, w_o, norm_g, w_mod, b_mod, w_up, w_down):
    batch, seq, _ = x.shape
    depth = w_in.shape[0]
    rows = batch * seq
    tm = min(512, seq)
    qs = min(128, seq)
    tq = min(4 * qs, seq)
    tk = min(512, seq)

    mod = _modulation(c, w_mod, b_mod).reshape(depth, batch, N_MOD, D_MODEL)
    cos_t, sin_t = _rope_tables(positions)
    x2 = x.reshape(rows, D_MODEL)
    w_in_b, w_o_b, w_up_b, w_down_b = _pack_in_proj(w_in), w_o.astype(BF16), w_up.astype(BF16), w_down.astype(BF16)
    for l in range(depth):
        conv, n_q, n_kv, narrow, l_qk, l_v, l_g, m_uv = _in_projection(
            x2, mod[l], norm_g[l], w_in_b, l, seq, tm)
        qp, qr, ks, vs, kw, vw = _nsa_prep(n_q, n_kv, cos_t, sin_t, batch, seq, tm)
        kc, vc = _nsa_compress(n_kv, cmp_pos[l], cmp_w1[l], cmp_w2[l], batch, seq)
        nsa = _nsa_attention(qp, qr, kc, vc, ks, vs, kw, vw, narrow, batch, seq, tq, qs, tk)
        gla = _gla(l_qk, l_v, l_g, narrow, gla_gate_w2[l], gla_gate_b[l], batch, seq, tm)
        gmlp = _gmlp(m_uv, gmlp_ws[l], gmlp_b[l], rows, tm)
        x2 = _layer_tail(conv, nsa, gla, gmlp, conv_w[l], grp_gain[l], w_o_b, w_up_b, w_down_b, l,
                         x2, mod[l], norm_g[l], seq, tm)
    return x2.reshape(batch, seq, D_MODEL)
```

```python
import functools

import numpy as np
import jax
import jax.numpy as jnp
from jax import lax
from jax.experimental import pallas as pl
from jax.experimental.pallas import tpu as pltpu

D_MODEL = 1024
N_MIXERS = 4
GROUP_WIDTH = D_MODEL // N_MIXERS
HEAD_DIM = 64
N_GROUP_HEADS = GROUP_WIDTH // HEAD_DIM
D_FF = 4 * D_MODEL
N_MOD = 6
RMS_EPS = 1e-6
NEG_INF = -1e30

ROPE_THETA = 500000.0
ROPE_DIM = HEAD_DIM // 4
ROPE_HALF = ROPE_DIM // 2

CONV_WIDTH = 3

CMP_BLOCK = 32
CMP_STRIDE = 16
CMP_HIDDEN = 2 * HEAD_DIM
SLC_BLOCK = 64
N_SLC = 16
N_INIT_BLOCKS = 1
N_LOCAL_BLOCKS = 2
FORCE_BONUS = 1e4
WINDOW = 512
N_BRANCH = 3
BLK_GROUP = 16
LOG2E = 1.4426950408889634
V_ROWS = HEAD_DIM + 16

GLA_DK = HEAD_DIM // 2
GLA_DV = HEAD_DIM
GLA_RANK = 16
GLA_TAU = 16.0
GLA_CHUNK = 64

GMLP_CHUNK = 128

LANE = 128
VMEM_LIMIT = 48 * 1024 * 1024

F32 = jnp.float32
BF16 = jnp.bfloat16

N_GATE_COLS = N_BRANCH * N_GROUP_HEADS
SEG_WIDTHS = (3 * GROUP_WIDTH,
              GROUP_WIDTH,
              2 * N_BRANCH * HEAD_DIM,
              LANE,
              2 * N_GROUP_HEADS * GLA_DK,
              GROUP_WIDTH,
              GROUP_WIDTH,
              2 * GROUP_WIDTH)
SEG_DTYPES = (BF16, BF16, F32, BF16, BF16, BF16, BF16, BF16)
SEG_OFFS = tuple(int(v) for v in np.cumsum((0,) + SEG_WIDTHS))
D_IN_PAD = SEG_OFFS[-1]
DOT_GROUPS = ((0,), (1,), (2, 3), (4,), (5,), (6,), (7,))


def _cparams(sem):
    return pltpu.CompilerParams(dimension_semantics=sem, vmem_limit_bytes=VMEM_LIMIT)


def _bdot(a, b):
    return jnp.dot(a.astype(BF16), b.astype(BF16), preferred_element_type=F32)


def _dot_nt(a, b):
    return lax.dot_general(a.astype(BF16), b.astype(BF16), (((1,), (1,)), ((), ())),
                           preferred_element_type=F32)


def _split2(a):
    hi = a.astype(BF16)
    lo = (a - hi.astype(F32)).astype(BF16)
    return hi, lo


def _split3(a):
    hi = a.astype(BF16)
    r = a - hi.astype(F32)
    mid = r.astype(BF16)
    lo = (r - mid.astype(F32)).astype(BF16)
    return hi, mid, lo


def _dot_lhs3(a, b_exact):
    hi, mid, lo = _split3(a)
    b = b_exact.astype(BF16)
    d = lambda p: jnp.dot(p, b, preferred_element_type=F32)
    return d(hi) + d(mid) + d(lo)


def _dot_rhs3(a_exact, b):
    hi, mid, lo = _split3(b)
    a = a_exact.astype(BF16)
    d = lambda p: jnp.dot(a, p, preferred_element_type=F32)
    return d(hi) + d(mid) + d(lo)


def _dot_f32(a, b):
    ah, al = _split2(a)
    bh, bl = _split2(b)
    d = lambda p, q: jnp.dot(p, q, preferred_element_type=F32)
    return d(ah, bh) + d(al, bh) + d(ah, bl)


def _rms(x):
    return x * lax.rsqrt(jnp.mean(x * x, axis=-1, keepdims=True) + RMS_EPS)


def _sigmoid(x):
    return 1.0 / (1.0 + jnp.exp(-x))


def _group_mean_matrix(width, group):
    r = lax.broadcasted_iota(jnp.int32, (width, width), 0) // group
    c = lax.broadcasted_iota(jnp.int32, (width, width), 1) // group
    return jnp.where(r == c, 1.0 / group, 0.0).astype(BF16)


def _mod_kernel(c_ref, w_ref, b_ref, o_ref):
    c = c_ref[...]
    cond = c * _sigmoid(c)
    o_ref[0] = _dot_f32(cond, w_ref[0]) + b_ref[0]


def _modulation(c, w_mod, b_mod):
    depth, d, n = w_mod.shape
    b = c.shape[0]
    tn = D_MODEL
    return pl.pallas_call(
        _mod_kernel,
        out_shape=jax.ShapeDtypeStruct((depth, b, n), F32),
        grid=(depth, n // tn),
        in_specs=[pl.BlockSpec((b, d), lambda l, j: (0, 0)),
                  pl.BlockSpec((1, d, tn), lambda l, j: (l, 0, j)),
                  pl.BlockSpec((1, 1, tn), lambda l, j: (l, 0, j))],
        out_specs=pl.BlockSpec((1, b, tn), lambda l, j: (l, 0, j)),
        compiler_params=_cparams(("parallel", "parallel")),
        name="adaln_mod",
    )(c, w_mod, b_mod.reshape(depth, 1, n))


def _rope_table_kernel(pos_ref, freq_ref, sign_ref, cos_ref, sin_ref):
    ang = pos_ref[...].astype(F32) * freq_ref[...]
    cos_ref[...] = jnp.cos(ang)
    sin_ref[...] = jnp.sin(ang) * sign_ref[...]


def _rope_tables(positions):
    b, s = positions.shape
    rows = b * s
    tr = min(1024, rows)
    inv_freq = ROPE_THETA ** (-jnp.arange(0, ROPE_DIM, 2, dtype=F32) / ROPE_DIM)
    lane = np.arange(LANE) % HEAD_DIM
    freq = jnp.where(lane < ROPE_DIM, inv_freq[lane % ROPE_HALF], 0.0).astype(F32)
    sign = jnp.asarray(np.where(lane < ROPE_HALF, -1.0, 1.0), dtype=F32)
    shape = jax.ShapeDtypeStruct((rows, LANE), F32)
    return pl.pallas_call(
        _rope_table_kernel,
        out_shape=(shape, shape),
        grid=(rows // tr,),
        in_specs=[pl.BlockSpec((tr, 1), lambda i: (i, 0)),
                  pl.BlockSpec((1, LANE), lambda i: (0, 0)),
                  pl.BlockSpec((1, LANE), lambda i: (0, 0))],
        out_specs=(pl.BlockSpec((tr, LANE), lambda i: (i, 0)),
                   pl.BlockSpec((tr, LANE), lambda i: (i, 0))),
        compiler_params=_cparams(("parallel",)),
        name="rope_tables",
    )(positions.reshape(rows, 1), freq.reshape(1, -1), sign.reshape(1, -1))


def _inproj_kernel(x_ref, mod_ref, g_ref, w_ref, *out_refs):
    tm = x_ref.shape[0]
    for r in (slice(0, tm // 2), slice(tm // 2, tm)):
        h = _rms(x_ref[r, :]) * g_ref[0:1, :]
        hb = (h * (1.0 + mod_ref[0, 1:2, :]) + mod_ref[0, 0:1, :]).astype(BF16)
        for group in DOT_GROUPS:
            lo, hi = SEG_OFFS[group[0]], SEG_OFFS[group[-1] + 1]
            z = jnp.dot(hb, w_ref[0, :, lo:hi], preferred_element_type=F32)
            for k in group:
                o_ref = out_refs[k]
                o_ref[r, :] = z[:, SEG_OFFS[k] - lo:SEG_OFFS[k + 1] - lo].astype(o_ref.dtype)


def _in_projection(x2, mod_l, norm_g_l, w_packed, layer, seq, tm):
    rows = x2.shape[0]
    tpb = seq // tm
    outs = tuple(jax.ShapeDtypeStruct((rows, w), dt) for w, dt in zip(SEG_WIDTHS, SEG_DTYPES))
    return pl.pallas_call(
        _inproj_kernel,
        out_shape=outs,
        grid=(rows // tm,),
        in_specs=[pl.BlockSpec((tm, D_MODEL), lambda i: (i, 0)),
                  pl.BlockSpec((1, N_MOD, D_MODEL), lambda i: (i // tpb, 0, 0)),
                  pl.BlockSpec((4, D_MODEL), lambda i: (0, 0)),
                  pl.BlockSpec((1, D_MODEL, D_IN_PAD), lambda i: (layer, 0, 0))],
        out_specs=tuple(pl.BlockSpec((tm, w), lambda i: (i, 0)) for w in SEG_WIDTHS),
        compiler_params=_cparams(("parallel",)),
        name="in_proj",
    )(x2, mod_l, norm_g_l, w_packed)


def _rope(x, cos, sin):
    w = x.shape[-1]
    lane = lax.broadcasted_iota(jnp.int32, x.shape, 1) % HEAD_DIM
    swapped = jnp.where(lane < ROPE_HALF,
                        pltpu.roll(x, w - ROPE_HALF, axis=1),
                        pltpu.roll(x, ROPE_HALF, axis=1))
    return x * cos + swapped * sin


def _nsa_prep_kernel(q_ref, kv_ref, cos_ref, sin_ref,
                     qp_ref, qr_ref, ks_ref, vs_ref, kw_ref, vw_ref):
    tp = cos_ref.shape[0]
    cos = cos_ref[...]
    sin = sin_ref[...]
    q = q_ref[...].astype(F32) * (HEAD_DIM ** -0.5 * LOG2E)
    qp_ref[0] = q.T.astype(BF16)
    reps = GROUP_WIDTH // LANE
    qr_ref[0] = _rope(q, jnp.concatenate([cos] * reps, axis=1),
                      jnp.concatenate([sin] * reps, axis=1)).T.astype(BF16)
    lane = lax.broadcasted_iota(jnp.int32, (tp, LANE), 1)
    is_k = lane < HEAD_DIM
    cos_kv = jnp.where(is_k, cos, 1.0)
    sin_kv = jnp.where(is_k, sin, 0.0)
    slc = _rope(kv_ref[:, LANE:2 * LANE], cos_kv, sin_kv)
    win = _rope(kv_ref[:, 2 * LANE:3 * LANE], cos_kv, sin_kv)
    blk = ((pl.program_id(1) * tp + lax.broadcasted_iota(jnp.int32, (tp, LANE), 0)) // SLC_BLOCK) % BLK_GROUP
    onehot = jnp.where(lane - HEAD_DIM == blk, 1.0, 0.0)
    ks_ref[0] = jnp.where(is_k, slc, onehot).astype(BF16)
    ones_rows = jnp.where(lax.broadcasted_iota(jnp.int32, (V_ROWS - HEAD_DIM, tp), 0) == 0, 1.0, 0.0)
    vs_ref[0] = jnp.concatenate([slc.T[HEAD_DIM:, :], ones_rows], axis=0).astype(BF16)
    kw_ref[0] = win[:, :HEAD_DIM].astype(BF16)
    vw_ref[0] = jnp.concatenate([win.T[HEAD_DIM:, :], ones_rows], axis=0).astype(BF16)


def _nsa_prep(q, kv, cos_t, sin_t, batch, seq, tp):
    tpb = seq // tp
    row = lambda b, i: (b * tpb + i, 0)
    qs = jax.ShapeDtypeStruct((batch, GROUP_WIDTH, seq), BF16)
    ks = jax.ShapeDtypeStruct((batch, seq, LANE), BF16)
    kw = jax.ShapeDtypeStruct((batch, seq, HEAD_DIM), BF16)
    vs = jax.ShapeDtypeStruct((batch, V_ROWS, seq), BF16)
    q_spec = pl.BlockSpec((1, GROUP_WIDTH, tp), lambda b, i: (b, 0, i))
    ks_spec = pl.BlockSpec((1, tp, LANE), lambda b, i: (b, i, 0))
    kw_spec = pl.BlockSpec((1, tp, HEAD_DIM), lambda b, i: (b, i, 0))
    v_spec = pl.BlockSpec((1, V_ROWS, tp), lambda b, i: (b, 0, i))
    return pl.pallas_call(
        _nsa_prep_kernel,
        out_shape=(qs, qs, ks, vs, kw, vs),
        grid=(batch, tpb),
        in_specs=[pl.BlockSpec((tp, GROUP_WIDTH), row),
                  pl.BlockSpec((tp, 2 * N_BRANCH * HEAD_DIM), row),
                  pl.BlockSpec((tp, LANE), row),
                  pl.BlockSpec((tp, LANE), row)],
        out_specs=(q_spec, q_spec, ks_spec, v_spec, kw_spec, v_spec),
        compiler_params=_cparams(("parallel", "parallel")),
        name="nsa_prep",
    )(q, kv, cos_t, sin_t)


def _nsa_compress_kernel(kv_ref, pos_ref, w1_ref, w2_ref, kc_ref, vc_ref, *, n_half):
    half = CMP_BLOCK // 2
    assert half == CMP_STRIDE
    chunks = [kv_ref[pl.ds(r, n_half, stride=CMP_STRIDE), :] for r in range(half)]
    outs = []
    for j in range(2):
        first = jnp.zeros((n_half, CMP_HIDDEN), F32)
        second = jnp.zeros((n_half, CMP_HIDDEN), F32)
        for r in range(half):
            t = chunks[r][:, j * HEAD_DIM:(j + 1) * HEAD_DIM]
            first += _bdot(t + pos_ref[j, r:r + 1, :],
                           w1_ref[j, r * HEAD_DIM:(r + 1) * HEAD_DIM, :])
            second += _bdot(t + pos_ref[j, half + r:half + r + 1, :],
                            w1_ref[j, (half + r) * HEAD_DIM:(half + r + 1) * HEAD_DIM, :])
        pre = first + pltpu.roll(second, n_half - 1, axis=0)
        row = lax.broadcasted_iota(jnp.int32, pre.shape, 0)
        pre = jnp.where(row < n_half - 1, pre, 0.0)
        hid = pre * _sigmoid(pre)
        outs.append(_bdot(hid, w2_ref[j]))
    kc_ref[0] = outs[0].astype(BF16)
    vc_ref[0] = jnp.concatenate(outs, axis=1).T[HEAD_DIM:, :].astype(BF16)


def _nsa_compress(kv, cmp_pos_l, cmp_w1_l, cmp_w2_l, batch, seq):
    n_half = seq // CMP_STRIDE
    return pl.pallas_call(
        functools.partial(_nsa_compress_kernel, n_half=n_half),
        out_shape=(jax.ShapeDtypeStruct((batch, n_half, HEAD_DIM), BF16),
                   jax.ShapeDtypeStruct((batch, HEAD_DIM, n_half), BF16)),
        grid=(batch,),
        in_specs=[pl.BlockSpec((seq, LANE), lambda b: (b, 0)),
                  pl.BlockSpec((2, CMP_BLOCK, HEAD_DIM), lambda b: (0, 0, 0)),
                  pl.BlockSpec((2, CMP_BLOCK * HEAD_DIM, CMP_HIDDEN), lambda b: (0, 0, 0)),
                  pl.BlockSpec((2, CMP_HIDDEN, HEAD_DIM), lambda b: (0, 0, 0))],
        out_specs=(pl.BlockSpec((1, n_half, HEAD_DIM), lambda b: (b, 0, 0)),
                   pl.BlockSpec((1, HEAD_DIM, n_half), lambda b: (b, 0, 0))),
        compiler_params=_cparams(("parallel",)),
        name="nsa_compress",
    )(kv, cmp_pos_l, cmp_w1_l, cmp_w2_l)


def _nsa_attn_kernel(qp_ref, qr_ref, kc_ref, vc_ref, ks_ref, vs_ref, kw_ref, vw_ref, gate_ref,
                     ovl_ref, o_ref, sel_ref, s_ref,
                     *, tq, qs, tk, seq, n_top, win_len):
    H = N_GROUP_HEADS
    W = H * qs
    G = tq // qs
    n_cmp_pad = kc_ref.shape[1]
    n_cmp = n_cmp_pad - 1
    n_blk = seq // SLC_BLOCK
    blk_per_tile = tk // SLC_BLOCK
    t_step = pl.program_id(1) * tq

    def heads_on_lanes(ref, g):
        return jnp.concatenate([ref[0, h * HEAD_DIM:(h + 1) * HEAD_DIM, g * qs:(g + 1) * qs]
                                for h in range(H)], axis=1)

    def tile_heads(a):
        return jnp.concatenate([a] * H, axis=1)

    t_sub = [t_step + g * qs for g in range(G)]
    q_rot = [heads_on_lanes(qr_ref, g) for g in range(G)]
    w_start = [pl.multiple_of(jnp.maximum(t0 + qs - win_len, 0), qs) for t0 in t_sub]
    s_cmp = [jnp.dot(kc_ref[0], heads_on_lanes(qp_ref, g), preferred_element_type=F32) for g in range(G)]
    s_win = [jnp.dot(kw_ref[0, pl.ds(w0, win_len), :], q, preferred_element_type=F32)
             for w0, q in zip(w_start, q_rot)]

    p_cmp = []
    for t0, s in zip(t_sub, s_cmp):
        n_c = lax.broadcasted_iota(jnp.int32, (n_cmp_pad, qs), 0)
        t_c = t0 + lax.broadcasted_iota(jnp.int32, (n_cmp_pad, qs), 1)
        bias_c = jnp.where((n_c * CMP_STRIDE + CMP_BLOCK - 1 <= t_c) & (n_c < n_cmp), 0.0, NEG_INF)
        s_c = s + tile_heads(bias_c)
        e_c = jnp.exp2(s_c - jnp.max(s_c, axis=0, keepdims=True))
        any_valid = jnp.where(t_c[0:1, :] >= CMP_BLOCK - 1, 1.0, 0.0)
        p_cmp.append(e_c * (tile_heads(any_valid) / jnp.sum(e_c, axis=0, keepdims=True)))
    o_cmp = [jnp.dot(vc_ref[0], p_c.astype(BF16), preferred_element_type=F32) for p_c in p_cmp]

    p_slc = []
    for p_c in p_cmp:
        p_sum = p_c[:, 0:qs]
        for h in range(1, H):
            p_sum = p_sum + p_c[:, h * qs:(h + 1) * qs]
        ps_hi, ps_lo = _split2(p_sum)
        p_slc.append(jnp.dot(ovl_ref[...], ps_hi, preferred_element_type=F32)
                     + jnp.dot(ovl_ref[...], ps_lo, preferred_element_type=F32))

    for g, (t0, imp) in enumerate(zip(t_sub, p_slc)):
        j_s = lax.broadcasted_iota(jnp.int32, (n_blk, qs), 0)
        cur = (t0 + lax.broadcasted_iota(jnp.int32, (n_blk, qs), 1)) // SLC_BLOCK
        forced = (j_s < N_INIT_BLOCKS) | ((cur - j_s >= 0) & (cur - j_s < N_LOCAL_BLOCKS))
        score = jnp.where(j_s <= cur, imp + jnp.where(forced, FORCE_BONUS, 0.0), NEG_INF)
        rank = jnp.zeros((n_blk, qs), F32)
        for jp in range(n_blk):
            row = score[jp:jp + 1, :]
            rank = rank + jnp.where(j_s > jp, jnp.where(row >= score, 1.0, 0.0),
                                    jnp.where(row > score, 1.0, 0.0))
        sel_ref[g] = jnp.where(rank < n_top, 1.0, 0.0)

    pad_rows = jnp.zeros((LANE - HEAD_DIM - BLK_GROUP, W), BF16)

    def augmented_queries(g, group):
        rows = sel_ref[g, pl.ds(pl.multiple_of(group * BLK_GROUP, BLK_GROUP), BLK_GROUP), :]
        bias = tile_heads((rows - 1.0) * -NEG_INF).astype(BF16)
        return jnp.concatenate([q_rot[g], bias, pad_rows], axis=0)

    t_blk0 = t_step // SLC_BLOCK
    strip = 2 * qs
    n_strip = W // strip
    th = tk // 2

    def score_strips(kt, half):
        k0 = pl.multiple_of(kt * tk + half * th, th)
        keys = ks_ref[0, pl.ds(k0, th), :]
        for g in range(G):
            q_aug = augmented_queries(g, kt // (BLK_GROUP // blk_per_tile))
            for c in range(n_strip):
                s_ref[half, g * n_strip + c] = jnp.dot(keys, q_aug[:, c * strip:(c + 1) * strip],
                                                       preferred_element_type=F32)

    t_al = pl.multiple_of(t_step, tq)
    s_diag = [jnp.dot(ks_ref[0, pl.ds(t_al, (g + 1) * qs), :], augmented_queries(g, t_blk0 // BLK_GROUP),
                      preferred_element_type=F32) for g in range(G)]
    score_strips(0, 0)

    o_win = []
    for t0, w0, s in zip(t_sub, w_start, s_win):
        diff = (t0 + lax.broadcasted_iota(jnp.int32, (win_len, qs), 1)
                - w0 - lax.broadcasted_iota(jnp.int32, (win_len, qs), 0))
        s_w = s + tile_heads(jnp.where((diff >= 0) & (diff < WINDOW), 0.0, NEG_INF))
        e_w = jnp.exp2((s_w - jnp.max(s_w, axis=0, keepdims=True)).astype(BF16))
        o_w = jnp.dot(vw_ref[0, :, pl.ds(w0, win_len)], e_w, preferred_element_type=F32)
        o_win.append(o_w[:HEAD_DIM] / o_w[HEAD_DIM:HEAD_DIM + 1])

    init = []
    for g in range(G):
        n_diag = (g + 1) * qs
        causal = (lax.broadcasted_iota(jnp.int32, (n_diag, qs), 0)
                  <= g * qs + lax.broadcasted_iota(jnp.int32, (n_diag, qs), 1))
        s = s_diag[g] + tile_heads(jnp.where(causal, 0.0, NEG_INF))
        m_new = jnp.max(s, axis=0, keepdims=True)
        p = jnp.exp2((s - m_new).astype(BF16))
        acc = jnp.dot(vs_ref[0, :, pl.ds(t_al, n_diag)], p, preferred_element_type=F32)
        for c in range(n_strip):
            cols = slice(c * strip, (c + 1) * strip)
            init.append((m_new[:, cols], acc[:, cols]))

    def softmax_strips(kt, half, stats):
        k0 = pl.multiple_of(kt * tk + half * th, th)
        vals = vs_ref[0, :, pl.ds(k0, th)]
        out = []
        for c, (m_old, acc_old) in enumerate(stats):
            s = s_ref[half, c]
            m_new = jnp.maximum(m_old, jnp.max(s, axis=0, keepdims=True))
            p = jnp.exp2((s - m_new).astype(BF16))
            alpha = jnp.exp2(m_old - m_new)
            out.append((m_new, alpha * acc_old + jnp.dot(vals, p, preferred_element_type=F32)))
        return tuple(out)

    n_tiles = t_step // tk

    def key_tile(kt, stats):
        score_strips(kt, 1)
        stats = softmax_strips(kt, 0, stats)
        score_strips(jnp.minimum(kt + 1, n_tiles - 1), 0)
        return softmax_strips(kt, 1, stats)

    final = lax.fori_loop(0, n_tiles, key_tile, tuple(init))

    for g in range(G):
        gate = _sigmoid(gate_ref[g * qs:(g + 1) * qs, :].astype(F32).T)
        o_sel = jnp.concatenate([acc_fin[:HEAD_DIM] / acc_fin[HEAD_DIM:HEAD_DIM + 1]
                                 for _, acc_fin in final[g * n_strip:(g + 1) * n_strip]], axis=1)
        outs = []
        for h in range(H):
            c = h * N_BRANCH
            cols = slice(h * qs, (h + 1) * qs)
            outs.append(gate[c:c + 1, :] * o_cmp[g][:, cols] + gate[c + 1:c + 2, :] * o_sel[:, cols]
                        + gate[c + 2:c + 3, :] * o_win[g][:, cols])
        o_ref[g * qs:(g + 1) * qs, :] = jnp.concatenate(outs, axis=0).T.astype(o_ref.dtype)


def _cmp_to_slc_t(n_cmp_pad, n_blk):
    n_cmp = n_cmp_pad - 1
    cs = np.arange(n_cmp) * CMP_STRIDE
    ss = np.arange(n_blk) * SLC_BLOCK
    ov = np.minimum(cs[:, None] + CMP_BLOCK, ss[None, :] + SLC_BLOCK) - np.maximum(cs[:, None], ss[None, :])
    m = np.zeros((n_cmp_pad, n_blk), np.float32)
    m[:n_cmp] = np.clip(ov, 0, None) / CMP_STRIDE
    return jnp.asarray(m.T, dtype=BF16)


def _nsa_attention(qp, qr, kc, vc, ks, vs, kw, vw, gates, batch, seq, tq, qs, tk):
    n_blk = seq // SLC_BLOCK
    n_cmp_pad = kc.shape[1]
    n_top = min(N_SLC, n_blk)
    win_len = min(WINDOW + qs, seq)
    tpb = seq // tq
    G = tq // qs
    W = N_GROUP_HEADS * qs
    full = lambda shape: pl.BlockSpec((1,) + shape, lambda b, i: (b, 0, 0))
    q_spec = pl.BlockSpec((1, GROUP_WIDTH, tq), lambda b, i: (b, 0, i))
    return pl.pallas_call(
        functools.partial(_nsa_attn_kernel, tq=tq, qs=qs, tk=tk, seq=seq, n_top=n_top, win_len=win_len),
        out_shape=jax.ShapeDtypeStruct((batch * seq, GROUP_WIDTH), BF16),
        grid=(batch, tpb),
        in_specs=[q_spec, q_spec,
                  full((n_cmp_pad, HEAD_DIM)), full((HEAD_DIM, n_cmp_pad)),
                  full((seq, LANE)), full((V_ROWS, seq)),
                  full((seq, HEAD_DIM)), full((V_ROWS, seq)),
                  pl.BlockSpec((tq, LANE), lambda b, i: (b * tpb + i, 0)),
                  pl.BlockSpec((n_blk, n_cmp_pad), lambda b, i: (0, 0))],
        out_specs=pl.BlockSpec((tq, GROUP_WIDTH), lambda b, i: (b * tpb + i, 0)),
        scratch_shapes=[pltpu.VMEM((G, n_blk, qs), F32),
                        pltpu.VMEM((2, W * G // (2 * qs), tk // 2, 2 * qs), F32)],
        compiler_params=_cparams(("parallel", "parallel")),
        name="nsa_attn",
    )(qp, qr, kc, vc, ks, vs, kw, vw, gates, _cmp_to_slc_t(n_cmp_pad, n_blk))


def _gla_kernel(qk_ref, v_ref, g_ref, lr_ref, w2_ref, gb_ref, o_ref, state_ref, *, n_chunks):
    H, C = N_GROUP_HEADS, GLA_CHUNK
    DKW = H * GLA_DK
    DVW = H * GLA_DV

    @pl.when(pl.program_id(1) == 0)
    def _():
        state_ref[...] = jnp.zeros(state_ref.shape, F32)

    tri = (lax.broadcasted_iota(jnp.int32, (C, C), 0)
           >= lax.broadcasted_iota(jnp.int32, (C, C), 1))
    tri_b = jnp.where(tri, 1.0, 0.0).astype(BF16)
    qhead = lax.broadcasted_iota(jnp.int32, (C, DKW), 1) // GLA_DK
    vhead = lax.broadcasted_iota(jnp.int32, (C, DVW), 1) // GLA_DV
    causal = jnp.concatenate([tri] * H, axis=0)
    state_mask = (lax.broadcasted_iota(jnp.int32, (DKW, DVW), 0) // GLA_DK
                  == lax.broadcasted_iota(jnp.int32, (DKW, DVW), 1) // GLA_DV)

    pre = _dot_rhs3(lr_ref[...], w2_ref[...]) + gb_ref[...]
    log_a = (jnp.minimum(pre, 0.0) - jnp.log1p(jnp.exp(-jnp.abs(pre)))) / GLA_TAU
    q_all = qk_ref[:, 0:DKW].astype(F32) * (GLA_DK ** -0.5)
    k_all = qk_ref[:, DKW:2 * DKW].astype(F32)

    chunk_rows = [slice(ci * C, (ci + 1) * C) for ci in range(n_chunks)]
    bf = lambda a, w: jnp.dot(a, w, preferred_element_type=F32)
    la_hi, la_mid, la_lo = _split3(log_a)
    b_all = [bf(tri_b, la_hi[r]) + bf(tri_b, la_mid[r]) + bf(tri_b, la_lo[r]) for r in chunk_rows]
    v_all = [v_ref[r, :] for r in chunk_rows]
    q_stack, k_in, q_dec, k_out_t, decay = [], [], [], [], []
    for r, b in zip(chunk_rows, b_all):
        b_mid = b[C // 2:C // 2 + 1, :]
        b_last = b[C - 1:C, :]
        q, k = q_all[r], k_all[r]
        q_in = q * jnp.exp(b - b_mid)
        q_stack.append(jnp.concatenate([jnp.where(qhead == h, q_in, 0.0) for h in range(H)],
                                       axis=0).astype(BF16))
        k_in.append((k * jnp.exp(b_mid - b)).astype(BF16))
        q_dec.append((q * jnp.exp(b)).astype(BF16))
        k_out_t.append((k * jnp.exp(b_last - b)).T.astype(BF16))
        decay.append(jnp.exp(jnp.broadcast_to(b_last, (8, DKW))).T[:, 0:1])
    att = [lax.dot_general(qs_, ki_, (((1,), (1,)), ((), ())), preferred_element_type=F32)
           for qs_, ki_ in zip(q_stack, k_in)]
    att = [jnp.where(causal, a, 0.0).astype(BF16) for a in att]
    o_stack = [bf(a, v) for a, v in zip(att, v_all)]
    kv_new = [jnp.where(state_mask, bf(kt_, v), 0.0) for kt_, v in zip(k_out_t, v_all)]
    o_intra = []
    for os_ in o_stack:
        o = jnp.where(vhead == 0, os_[0:C], 0.0)
        for h in range(1, H):
            o = o + jnp.where(vhead == h, os_[h * C:(h + 1) * C], 0.0)
        o_intra.append(o)

    state = state_ref[...]
    o_chunks = []
    for ci in range(n_chunks):
        o_chunks.append(o_intra[ci] + bf(q_dec[ci], state.astype(BF16)))
        state = decay[ci] * state + kv_new[ci]
    state_ref[...] = state

    o = jnp.concatenate(o_chunks, axis=0)
    ms = _dot_lhs3(o * o, _group_mean_matrix(DVW, GLA_DV))
    gate = g_ref[...].astype(F32)
    o_ref[...] = (gate * _sigmoid(gate) * (o * lax.rsqrt(ms + RMS_EPS))).astype(o_ref.dtype)


def _gla(qk, v, g, lr, gate_w2_l, gate_b_l, batch, seq, tg):
    tpb = seq // tg
    row = lambda b, i: (b * tpb + i, 0)
    w2 = jnp.zeros((LANE, N_GROUP_HEADS * GLA_DK), F32).at[N_GATE_COLS:N_GATE_COLS + GLA_RANK].set(gate_w2_l)
    return pl.pallas_call(
        functools.partial(_gla_kernel, n_chunks=tg // GLA_CHUNK),
        out_shape=jax.ShapeDtypeStruct((batch * seq, GROUP_WIDTH), BF16),
        grid=(batch, tpb),
        in_specs=[pl.BlockSpec((tg, 2 * N_GROUP_HEADS * GLA_DK), row),
                  pl.BlockSpec((tg, GROUP_WIDTH), row),
                  pl.BlockSpec((tg, GROUP_WIDTH), row),
                  pl.BlockSpec((tg, LANE), row),
                  pl.BlockSpec((LANE, N_GROUP_HEADS * GLA_DK), lambda b, i: (0, 0)),
                  pl.BlockSpec((1, N_GROUP_HEADS * GLA_DK), lambda b, i: (0, 0))],
        out_specs=pl.BlockSpec((tg, GROUP_WIDTH), row),
        scratch_shapes=[pltpu.VMEM((N_GROUP_HEADS * GLA_DK, N_GROUP_HEADS * GLA_DV), F32)],
        compiler_params=_cparams(("parallel", "arbitrary")),
        name="gla",
    )(qk, v, g, lr, w2, gate_b_l.reshape(1, -1))


def _gmlp_kernel(uv_ref, w_ref, bias_ref, o_ref, *, n_chunks):
    T, G = GMLP_CHUNK, N_GROUP_HEADS
    gdim = GROUP_WIDTH // G
    gmean = _group_mean_matrix(GROUP_WIDTH, gdim)
    tri = (lax.broadcasted_iota(jnp.int32, (T, T), 0)
           >= lax.broadcasted_iota(jnp.int32, (T, T), 1))
    head = lax.broadcasted_iota(jnp.int32, (T, GROUP_WIDTH), 1) // gdim
    w = [jnp.where(tri, w_ref[g], 0.0).astype(BF16) for g in range(G)]
    v = uv_ref[:, GROUP_WIDTH:2 * GROUP_WIDTH]
    d = v.astype(F32) - jnp.dot(v, gmean, preferred_element_type=F32)
    vn = (d * lax.rsqrt(_dot_lhs3(d * d, gmean) + RMS_EPS)).astype(BF16)
    mixes = [[jnp.dot(w[g], vn[ci * T:(ci + 1) * T], preferred_element_type=F32) for g in range(G)]
             for ci in range(n_chunks)]
    for ci in range(n_chunks):
        rows = slice(ci * T, (ci + 1) * T)
        mixed = bias_ref[...]
        for g in range(G):
            mixed = mixed + jnp.where(head == g, mixes[ci][g], 0.0)
        o_ref[rows, :] = (uv_ref[rows, 0:GROUP_WIDTH].astype(F32) * mixed).astype(o_ref.dtype)


def _gmlp(uv, gmlp_ws_l, gmlp_b_l, rows, tc):
    gdim = GROUP_WIDTH // N_GROUP_HEADS
    bias = jnp.repeat(gmlp_b_l.T, gdim, axis=1)
    return pl.pallas_call(
        functools.partial(_gmlp_kernel, n_chunks=tc // GMLP_CHUNK),
        out_shape=jax.ShapeDtypeStruct((rows, GROUP_WIDTH), BF16),
        grid=(rows // tc,),
        in_specs=[pl.BlockSpec((tc, 2 * GROUP_WIDTH), lambda i: (i, 0)),
                  pl.BlockSpec((N_GROUP_HEADS, GMLP_CHUNK, GMLP_CHUNK), lambda i: (0, 0, 0)),
                  pl.BlockSpec((GMLP_CHUNK, GROUP_WIDTH), lambda i: (0, 0))],
        out_specs=pl.BlockSpec((tc, GROUP_WIDTH), lambda i: (i, 0)),
        compiler_params=_cparams(("parallel",)),
        name="gmlp",
    )(uv, gmlp_ws_l, bias)


def _tail_kernel(conv_ref, halo_ref, cw_ref, nsa_ref, gla_ref, gmlp_ref, gain_ref, wo_ref,
                 x_ref, mod_ref, g_ref, wu_ref, wd_ref, o_ref, *, tpb, tf):
    GW = GROUP_WIDTH
    tm = conv_ref.shape[0]
    nh = halo_ref.shape[0]
    z = conv_ref[:, 2 * GW:3 * GW].astype(F32) * conv_ref[:, 0:GW].astype(F32)
    zh = halo_ref[:, 2 * GW:3 * GW].astype(F32) * halo_ref[:, 0:GW].astype(F32)
    zh = jnp.where(pl.program_id(0) % tpb == 0, 0.0, zh)
    row = lax.broadcasted_iota(jnp.int32, (tm, GW), 0)
    y = cw_ref[CONV_WIDTH - 1:CONV_WIDTH, :] * z
    for back in range(1, CONV_WIDTH):
        shifted = pltpu.roll(z, back, axis=0)
        for r in range(back):
            shifted = jnp.where(row == r, zh[nh - back + r:nh - back + r + 1, :], shifted)
        y = y + cw_ref[CONV_WIDTH - 1 - back:CONV_WIDTH - back, :] * shifted
    conv_out = conv_ref[:, GW:2 * GW].astype(F32) * y

    halves = [slice(0, tm // 2), slice(tm // 2, tm)]
    x_mid = []
    for r in halves:
        groups = (conv_out[r], nsa_ref[r, :], gla_ref[r, :], gmlp_ref[r, :])
        acc = jnp.zeros((tm // 2, D_MODEL), F32)
        for k, o in enumerate(groups):
            n = (_rms(o.astype(F32)) * gain_ref[k:k + 1, :]).astype(BF16)
            acc = acc + jnp.dot(n, wo_ref[0, k * GW:(k + 1) * GW, :], preferred_element_type=F32)
        x_mid.append(x_ref[r, :] + mod_ref[0, 2:3, :] * (_rms(acc) * g_ref[1:2, :]))

    hb = [((_rms(x) * g_ref[2:3, :]) * (1.0 + mod_ref[0, 4:5, :]) + mod_ref[0, 3:4, :]).astype(BF16)
          for x in x_mid]
    acc = [jnp.zeros((tm // 2, D_MODEL), F32) for _ in halves]
    for f in range(D_FF // tf):
        up = [jnp.dot(h, wu_ref[0, :, f * tf:(f + 1) * tf], preferred_element_type=F32) for h in hb]
        for i, u in enumerate(up):
            u = jnp.maximum(u, 0.0)
            acc[i] = acc[i] + jnp.dot((u * u).astype(BF16), wd_ref[0, f * tf:(f + 1) * tf, :],
                                      preferred_element_type=F32)
    for r, x, a in zip(halves, x_mid, acc):
        o_ref[r, :] = x + mod_ref[0, 5:6, :] * (_rms(a) * g_ref[3:4, :])


def _layer_tail(conv, nsa, gla, gmlp, conv_w_l, grp_gain_l, wo_all, wu_all, wd_all, layer,
                x2, mod_l, norm_g_l, seq, tm):
    rows = x2.shape[0]
    tpb = seq // tm
    halo = 16
    hpt = tm // halo
    gw = lambda i: (i, 0)
    const = lambda i: (0, 0)
    weight = lambda shape: pl.BlockSpec((1,) + shape, lambda i: (layer, 0, 0),
                                        pipeline_mode=pl.Buffered(1))
    return pl.pallas_call(
        functools.partial(_tail_kernel, tpb=tpb, tf=D_MODEL),
        out_shape=jax.ShapeDtypeStruct((rows, D_MODEL), F32),
        grid=(rows // tm,),
        in_specs=[pl.BlockSpec((tm, 3 * GROUP_WIDTH), gw),
                  pl.BlockSpec((halo, 3 * GROUP_WIDTH), lambda i: (jnp.maximum(i * hpt - 1, 0), 0)),
                  pl.BlockSpec((CONV_WIDTH, GROUP_WIDTH), const),
                  pl.BlockSpec((tm, GROUP_WIDTH), gw),
                  pl.BlockSpec((tm, GROUP_WIDTH), gw),
                  pl.BlockSpec((tm, GROUP_WIDTH), gw),
                  pl.BlockSpec((N_MIXERS, GROUP_WIDTH), const),
                  weight((D_MODEL, D_MODEL)),
                  pl.BlockSpec((tm, D_MODEL), gw),
                  pl.BlockSpec((1, N_MOD, D_MODEL), lambda i: (i // tpb, 0, 0)),
                  pl.BlockSpec((4, D_MODEL), const),
                  weight((D_MODEL, D_FF)),
                  weight((D_FF, D_MODEL))],
        out_specs=pl.BlockSpec((tm, D_MODEL), gw),
        compiler_params=_cparams(("parallel",)),
        name="layer_tail",
    )(conv, conv, conv_w_l, nsa, gla, gmlp, grp_gain_l.reshape(N_MIXERS, GROUP_WIDTH), wo_all,
      x2, mod_l, norm_g_l, wu_all, wd_all)


def _pack_in_proj(w_in):
    offs = np.cumsum((0,) + (3 * GROUP_WIDTH, GROUP_WIDTH, 2 * N_BRANCH * HEAD_DIM, N_GATE_COLS,
                             2 * N_GROUP_HEADS * GLA_DK, GROUP_WIDTH, GROUP_WIDTH, GLA_RANK,
                             2 * GROUP_WIDTH))
    col = lambda a, b: w_in[..., offs[a]:offs[b]]
    narrow_pad = jnp.zeros(w_in.shape[:-1] + (LANE - N_GATE_COLS - GLA_RANK,), w_in.dtype)
    return jnp.concatenate([col(0, 3),
                            col(3, 4), col(7, 8), narrow_pad,
                            col(4, 7),
                            col(8, 9)], axis=-1).astype(BF16)


def kernel(x, c, positions, w_in, conv_w, cmp_pos, cmp_w1, cmp_w2, gla_gate_w2, gla_gate_b, gmlp_ws, gmlp_b, grp_gain, w_o, norm_g, w_mod, b_mod, w_up, w_down):
    batch, seq, _ = x.shape
    depth = w_in.shape[0]
    rows = batch * seq
    tm = min(512, seq)
    qs = min(128, seq)
    tq = min(4 * qs, seq)
    tk = min(512, seq)

    mod = _modulation(c, w_mod, b_mod).reshape(depth, batch, N_MOD, D_MODEL)
    cos_t, sin_t = _rope_tables(positions)
    x2 = x.reshape(rows, D_MODEL)
    w_in_b, w_o_b, w_up_b, w_down_b = _pack_in_proj(w_in), w_o.astype(BF16), w_up.astype(BF16), w_down.astype(BF16)
    for l in range(depth):
        conv, n_q, n_kv, narrow, l_qk, l_v, l_g, m_uv = _in_projection(
            x2, mod[l], norm_g[l], w_in_b, l, seq, tm)
        qp, qr, ks, vs, kw, vw = _nsa_prep(n_q, n_kv, cos_t, sin_t, batch, seq, tm)
        kc, vc = _nsa_compress(n_kv, cmp_pos[l], cmp_w1[l], cmp_w2[l], batch, seq)
        nsa = _nsa_attention(qp, qr, kc, vc, ks, vs, kw, vw, narrow, batch, seq, tq, qs, tk)
        gla = _gla(l_qk, l_v, l_g, narrow, gla_gate_w2[l], gla_gate_b[l], batch, seq, tm)
        gmlp = _gmlp(m_uv, gmlp_ws[l], gmlp_b[l], rows, tm)
        x2 = _layer_tail(conv, nsa, gla, gmlp, conv_w[l], grp_gain[l], w_o_b, w_up_b, w_down_b, l,
                         x2, mod[l], norm_g[l], seq, tm)
    return x2.reshape(batch, seq, D_MODEL)
```

```python
import functools

import numpy as np
import jax
import jax.numpy as jnp
from jax import lax
from jax.experimental import pallas as pl
from jax.experimental.pallas import tpu as pltpu

D_MODEL = 1024
N_MIXERS = 4
GROUP_WIDTH = D_MODEL // N_MIXERS
HEAD_DIM = 64
N_GROUP_HEADS = GROUP_WIDTH // HEAD_DIM
D_FF = 4 * D_MODEL
N_MOD = 6
RMS_EPS = 1e-6
NEG_INF = -1e30

ROPE_THETA = 500000.0
ROPE_DIM = HEAD_DIM // 4
ROPE_HALF = ROPE_DIM // 2

CONV_WIDTH = 3

CMP_BLOCK = 32
CMP_STRIDE = 16
CMP_HIDDEN = 2 * HEAD_DIM
SLC_BLOCK = 64
N_SLC = 16
N_INIT_BLOCKS = 1
N_LOCAL_BLOCKS = 2
FORCE_BONUS = 1e4
WINDOW = 512
N_BRANCH = 3
BLK_GROUP = 16
LOG2E = 1.4426950408889634
V_ROWS = HEAD_DIM + 16

GLA_DK = HEAD_DIM // 2
GLA_DV = HEAD_DIM
GLA_RANK = 16
GLA_TAU = 16.0
GLA_CHUNK = 64

GMLP_CHUNK = 128

LANE = 128
VMEM_LIMIT = 48 * 1024 * 1024

F32 = jnp.float32
BF16 = jnp.bfloat16

N_GATE_COLS = N_BRANCH * N_GROUP_HEADS
SEG_WIDTHS = (3 * GROUP_WIDTH,
              GROUP_WIDTH,
              2 * N_BRANCH * HEAD_DIM,
              LANE,
              2 * N_GROUP_HEADS * GLA_DK,
              GROUP_WIDTH,
              GROUP_WIDTH,
              2 * GROUP_WIDTH)
SEG_DTYPES = (BF16, BF16, F32, BF16, BF16, BF16, BF16, BF16)
SEG_OFFS = tuple(int(v) for v in np.cumsum((0,) + SEG_WIDTHS))
D_IN_PAD = SEG_OFFS[-1]
DOT_GROUPS = ((0,), (1,), (2, 3), (4,), (5,), (6,), (7,))


def _cparams(sem):
    return pltpu.CompilerParams(dimension_semantics=sem, vmem_limit_bytes=VMEM_LIMIT)


def _bdot(a, b):
    return jnp.dot(a.astype(BF16), b.astype(BF16), preferred_element_type=F32)


def _dot_nt(a, b):
    return lax.dot_general(a.astype(BF16), b.astype(BF16), (((1,), (1,)), ((), ())),
                           preferred_element_type=F32)


def _split2(a):
    hi = a.astype(BF16)
    lo = (a - hi.astype(F32)).astype(BF16)
    return hi, lo


def _split3(a):
    hi = a.astype(BF16)
    r = a - hi.astype(F32)
    mid = r.astype(BF16)
    lo = (r - mid.astype(F32)).astype(BF16)
    return hi, mid, lo


def _dot_lhs3(a, b_exact):
    hi, mid, lo = _split3(a)
    b = b_exact.astype(BF16)
    d = lambda p: jnp.dot(p, b, preferred_element_type=F32)
    return d(hi) + d(mid) + d(lo)


def _dot_rhs3(a_exact, b):
    hi, mid, lo = _split3(b)
    a = a_exact.astype(BF16)
    d = lambda p: jnp.dot(a, p, preferred_element_type=F32)
    return d(hi) + d(mid) + d(lo)


def _dot_f32(a, b):
    ah, al = _split2(a)
    bh, bl = _split2(b)
    d = lambda p, q: jnp.dot(p, q, preferred_element_type=F32)
    return d(ah, bh) + d(al, bh) + d(ah, bl)


def _rms(x):
    return x * lax.rsqrt(jnp.mean(x * x, axis=-1, keepdims=True) + RMS_EPS)


def _sigmoid(x):
    return 1.0 / (1.0 + jnp.exp(-x))


def _group_mean_matrix(width, group):
    r = lax.broadcasted_iota(jnp.int32, (width, width), 0) // group
    c = lax.broadcasted_iota(jnp.int32, (width, width), 1) // group
    return jnp.where(r == c, 1.0 / group, 0.0).astype(BF16)


def _mod_kernel(c_ref, w_ref, b_ref, o_ref):
    c = c_ref[...]
    cond = c * _sigmoid(c)
    o_ref[0] = _dot_f32(cond, w_ref[0]) + b_ref[0]


def _modulation(c, w_mod, b_mod):
    depth, d, n = w_mod.shape
    b = c.shape[0]
    tn = D_MODEL
    return pl.pallas_call(
        _mod_kernel,
        out_shape=jax.ShapeDtypeStruct((depth, b, n), F32),
        grid=(depth, n // tn),
        in_specs=[pl.BlockSpec((b, d), lambda l, j: (0, 0)),
                  pl.BlockSpec((1, d, tn), lambda l, j: (l, 0, j)),
                  pl.BlockSpec((1, 1, tn), lambda l, j: (l, 0, j))],
        out_specs=pl.BlockSpec((1, b, tn), lambda l, j: (l, 0, j)),
        compiler_params=_cparams(("parallel", "parallel")),
        name="adaln_mod",
    )(c, w_mod, b_mod.reshape(depth, 1, n))


def _rope_table_kernel(pos_ref, freq_ref, sign_ref, cos_ref, sin_ref):
    ang = pos_ref[...].astype(F32) * freq_ref[...]
    cos_ref[...] = jnp.cos(ang)
    sin_ref[...] = jnp.sin(ang) * sign_ref[...]


def _rope_tables(positions):
    b, s = positions.shape
    rows = b * s
    tr = min(1024, rows)
    inv_freq = ROPE_THETA ** (-jnp.arange(0, ROPE_DIM, 2, dtype=F32) / ROPE_DIM)
    lane = np.arange(LANE) % HEAD_DIM
    freq = jnp.where(lane < ROPE_DIM, inv_freq[lane % ROPE_HALF], 0.0).astype(F32)
    sign = jnp.asarray(np.where(lane < ROPE_HALF, -1.0, 1.0), dtype=F32)
    shape = jax.ShapeDtypeStruct((rows, LANE), F32)
    return pl.pallas_call(
        _rope_table_kernel,
        out_shape=(shape, shape),
        grid=(rows // tr,),
        in_specs=[pl.BlockSpec((tr, 1), lambda i: (i, 0)),
                  pl.BlockSpec((1, LANE), lambda i: (0, 0)),
                  pl.BlockSpec((1, LANE), lambda i: (0, 0))],
        out_specs=(pl.BlockSpec((tr, LANE), lambda i: (i, 0)),
                   pl.BlockSpec((tr, LANE), lambda i: (i, 0))),
        compiler_params=_cparams(("parallel",)),
        name="rope_tables",
    )(positions.reshape(rows, 1), freq.reshape(1, -1), sign.reshape(1, -1))


def _inproj_kernel(x_ref, mod_ref, g_ref, w_ref, *out_refs):
    h = _rms(x_ref[...]) * g_ref[0:1, :]
    hb = (h * (1.0 + mod_ref[0, 1:2, :]) + mod_ref[0, 0:1, :]).astype(BF16)
    for group in DOT_GROUPS:
        lo, hi = SEG_OFFS[group[0]], SEG_OFFS[group[-1] + 1]
        z = jnp.dot(hb, w_ref[0, :, lo:hi], preferred_element_type=F32)
        for k in group:
            o_ref = out_refs[k]
            o_ref[...] = z[:, SEG_OFFS[k] - lo:SEG_OFFS[k + 1] - lo].astype(o_ref.dtype)


def _in_projection(x2, mod_l, norm_g_l, w_packed, layer, seq, tm):
    rows = x2.shape[0]
    tpb = seq // tm
    outs = tuple(jax.ShapeDtypeStruct((rows, w), dt) for w, dt in zip(SEG_WIDTHS, SEG_DTYPES))
    return pl.pallas_call(
        _inproj_kernel,
        out_shape=outs,
        grid=(rows // tm,),
        in_specs=[pl.BlockSpec((tm, D_MODEL), lambda i: (i, 0)),
                  pl.BlockSpec((1, N_MOD, D_MODEL), lambda i: (i // tpb, 0, 0)),
                  pl.BlockSpec((4, D_MODEL), lambda i: (0, 0)),
                  pl.BlockSpec((1, D_MODEL, D_IN_PAD), lambda i: (layer, 0, 0))],
        out_specs=tuple(pl.BlockSpec((tm, w), lambda i: (i, 0)) for w in SEG_WIDTHS),
        compiler_params=_cparams(("parallel",)),
        name="in_proj",
    )(x2, mod_l, norm_g_l, w_packed)


def _rope(x, cos, sin):
    w = x.shape[-1]
    lane = lax.broadcasted_iota(jnp.int32, x.shape, 1) % HEAD_DIM
    swapped = jnp.where(lane < ROPE_HALF,
                        pltpu.roll(x, w - ROPE_HALF, axis=1),
                        pltpu.roll(x, ROPE_HALF, axis=1))
    return x * cos + swapped * sin


def _nsa_prep_kernel(q_ref, kv_ref, cos_ref, sin_ref,
                     qp_ref, qr_ref, ks_ref, vs_ref, kw_ref, vw_ref):
    tp = cos_ref.shape[0]
    cos = cos_ref[...]
    sin = sin_ref[...]
    q = q_ref[...].astype(F32) * (HEAD_DIM ** -0.5 * LOG2E)
    qp_ref[0] = q.T.astype(BF16)
    reps = GROUP_WIDTH // LANE
    qr_ref[0] = _rope(q, jnp.concatenate([cos] * reps, axis=1),
                      jnp.concatenate([sin] * reps, axis=1)).T.astype(BF16)
    lane = lax.broadcasted_iota(jnp.int32, (tp, LANE), 1)
    is_k = lane < HEAD_DIM
    cos_kv = jnp.where(is_k, cos, 1.0)
    sin_kv = jnp.where(is_k, sin, 0.0)
    slc = _rope(kv_ref[:, LANE:2 * LANE], cos_kv, sin_kv)
    win = _rope(kv_ref[:, 2 * LANE:3 * LANE], cos_kv, sin_kv)
    blk = ((pl.program_id(1) * tp + lax.broadcasted_iota(jnp.int32, (tp, LANE), 0)) // SLC_BLOCK) % BLK_GROUP
    onehot = jnp.where(lane - HEAD_DIM == blk, 1.0, 0.0)
    ks_ref[0] = jnp.where(is_k, slc, onehot).astype(BF16)
    ones_rows = jnp.where(lax.broadcasted_iota(jnp.int32, (V_ROWS - HEAD_DIM, tp), 0) == 0, 1.0, 0.0)
    vs_ref[0] = jnp.concatenate([slc.T[HEAD_DIM:, :], ones_rows], axis=0).astype(BF16)
    kw_ref[0] = win[:, :HEAD_DIM].astype(BF16)
    vw_ref[0] = jnp.concatenate([win.T[HEAD_DIM:, :], ones_rows], axis=0).astype(BF16)


def _nsa_prep(q, kv, cos_t, sin_t, batch, seq, tp):
    tpb = seq // tp
    row = lambda b, i: (b * tpb + i, 0)
    qs = jax.ShapeDtypeStruct((batch, GROUP_WIDTH, seq), BF16)
    ks = jax.ShapeDtypeStruct((batch, seq, LANE), BF16)
    kw = jax.ShapeDtypeStruct((batch, seq, HEAD_DIM), BF16)
    vs = jax.ShapeDtypeStruct((batch, V_ROWS, seq), BF16)
    q_spec = pl.BlockSpec((1, GROUP_WIDTH, tp), lambda b, i: (b, 0, i))
    ks_spec = pl.BlockSpec((1, tp, LANE), lambda b, i: (b, i, 0))
    kw_spec = pl.BlockSpec((1, tp, HEAD_DIM), lambda b, i: (b, i, 0))
    v_spec = pl.BlockSpec((1, V_ROWS, tp), lambda b, i: (b, 0, i))
    return pl.pallas_call(
        _nsa_prep_kernel,
        out_shape=(qs, qs, ks, vs, kw, vs),
        grid=(batch, tpb),
        in_specs=[pl.BlockSpec((tp, GROUP_WIDTH), row),
                  pl.BlockSpec((tp, 2 * N_BRANCH * HEAD_DIM), row),
                  pl.BlockSpec((tp, LANE), row),
                  pl.BlockSpec((tp, LANE), row)],
        out_specs=(q_spec, q_spec, ks_spec, v_spec, kw_spec, v_spec),
        compiler_params=_cparams(("parallel", "parallel")),
        name="nsa_prep",
    )(q, kv, cos_t, sin_t)


def _nsa_compress_kernel(kv_ref, pos_ref, w1_ref, w2_ref, kc_ref, vc_ref, *, n_half):
    half = CMP_BLOCK // 2
    assert half == CMP_STRIDE
    chunks = [kv_ref[pl.ds(r, n_half, stride=CMP_STRIDE), :] for r in range(half)]
    outs = []
    for j in range(2):
        first = jnp.zeros((n_half, CMP_HIDDEN), F32)
        second = jnp.zeros((n_half, CMP_HIDDEN), F32)
        for r in range(half):
            t = chunks[r][:, j * HEAD_DIM:(j + 1) * HEAD_DIM]
            first += _bdot(t + pos_ref[j, r:r + 1, :],
                           w1_ref[j, r * HEAD_DIM:(r + 1) * HEAD_DIM, :])
            second += _bdot(t + pos_ref[j, half + r:half + r + 1, :],
                            w1_ref[j, (half + r) * HEAD_DIM:(half + r + 1) * HEAD_DIM, :])
        pre = first + pltpu.roll(second, n_half - 1, axis=0)
        row = lax.broadcasted_iota(jnp.int32, pre.shape, 0)
        pre = jnp.where(row < n_half - 1, pre, 0.0)
        hid = pre * _sigmoid(pre)
        outs.append(_bdot(hid, w2_ref[j]))
    kc_ref[0] = outs[0].astype(BF16)
    vc_ref[0] = jnp.concatenate(outs, axis=1).T[HEAD_DIM:, :].astype(BF16)


def _nsa_compress(kv, cmp_pos_l, cmp_w1_l, cmp_w2_l, batch, seq):
    n_half = seq // CMP_STRIDE
    return pl.pallas_call(
        functools.partial(_nsa_compress_kernel, n_half=n_half),
        out_shape=(jax.ShapeDtypeStruct((batch, n_half, HEAD_DIM), BF16),
                   jax.ShapeDtypeStruct((batch, HEAD_DIM, n_half), BF16)),
        grid=(batch,),
        in_specs=[pl.BlockSpec((seq, LANE), lambda b: (b, 0)),
                  pl.BlockSpec((2, CMP_BLOCK, HEAD_DIM), lambda b: (0, 0, 0)),
                  pl.BlockSpec((2, CMP_BLOCK * HEAD_DIM, CMP_HIDDEN), lambda b: (0, 0, 0)),
                  pl.BlockSpec((2, CMP_HIDDEN, HEAD_DIM), lambda b: (0, 0, 0))],
        out_specs=(pl.BlockSpec((1, n_half, HEAD_DIM), lambda b: (b, 0, 0)),
                   pl.BlockSpec((1, HEAD_DIM, n_half), lambda b: (b, 0, 0))),
        compiler_params=_cparams(("parallel",)),
        name="nsa_compress",
    )(kv, cmp_pos_l, cmp_w1_l, cmp_w2_l)


def _nsa_attn_kernel(qp_ref, qr_ref, kc_ref, vc_ref, ks_ref, vs_ref, kw_ref, vw_ref, gate_ref,
                     ovl_ref, o_ref, sel_ref, s_ref,
                     *, tq, qs, tk, seq, n_top, win_len):
    H = N_GROUP_HEADS
    W = H * qs
    G = tq // qs
    n_cmp_pad = kc_ref.shape[1]
    n_cmp = n_cmp_pad - 1
    n_blk = seq // SLC_BLOCK
    blk_per_tile = tk // SLC_BLOCK
    t_step = pl.program_id(1) * tq

    def heads_on_lanes(ref, g):
        return jnp.concatenate([ref[0, h * HEAD_DIM:(h + 1) * HEAD_DIM, g * qs:(g + 1) * qs]
                                for h in range(H)], axis=1)

    def tile_heads(a):
        return jnp.concatenate([a] * H, axis=1)

    t_sub = [t_step + g * qs for g in range(G)]
    q_rot = [heads_on_lanes(qr_ref, g) for g in range(G)]
    w_start = [pl.multiple_of(jnp.maximum(t0 + qs - win_len, 0), qs) for t0 in t_sub]
    s_cmp = [jnp.dot(kc_ref[0], heads_on_lanes(qp_ref, g), preferred_element_type=F32) for g in range(G)]
    s_win = [jnp.dot(kw_ref[0, pl.ds(w0, win_len), :], q, preferred_element_type=F32)
             for w0, q in zip(w_start, q_rot)]

    p_cmp = []
    for t0, s in zip(t_sub, s_cmp):
        n_c = lax.broadcasted_iota(jnp.int32, (n_cmp_pad, qs), 0)
        t_c = t0 + lax.broadcasted_iota(jnp.int32, (n_cmp_pad, qs), 1)
        bias_c = jnp.where((n_c * CMP_STRIDE + CMP_BLOCK - 1 <= t_c) & (n_c < n_cmp), 0.0, NEG_INF)
        s_c = s + tile_heads(bias_c)
        e_c = jnp.exp2(s_c - jnp.max(s_c, axis=0, keepdims=True))
        any_valid = jnp.where(t_c[0:1, :] >= CMP_BLOCK - 1, 1.0, 0.0)
        p_cmp.append(e_c * (tile_heads(any_valid) / jnp.sum(e_c, axis=0, keepdims=True)))
    o_cmp = [jnp.dot(vc_ref[0], p_c.astype(BF16), preferred_element_type=F32) for p_c in p_cmp]

    p_slc = []
    for p_c in p_cmp:
        p_sum = p_c[:, 0:qs]
        for h in range(1, H):
            p_sum = p_sum + p_c[:, h * qs:(h + 1) * qs]
        ps_hi, ps_lo = _split2(p_sum)
        p_slc.append(jnp.dot(ovl_ref[...], ps_hi, preferred_element_type=F32)
                     + jnp.dot(ovl_ref[...], ps_lo, preferred_element_type=F32))

    for g, (t0, imp) in enumerate(zip(t_sub, p_slc)):
        j_s = lax.broadcasted_iota(jnp.int32, (n_blk, qs), 0)
        cur = (t0 + lax.broadcasted_iota(jnp.int32, (n_blk, qs), 1)) // SLC_BLOCK
        forced = (j_s < N_INIT_BLOCKS) | ((cur - j_s >= 0) & (cur - j_s < N_LOCAL_BLOCKS))
        score = jnp.where(j_s <= cur, imp + jnp.where(forced, FORCE_BONUS, 0.0), NEG_INF)
        rank = jnp.zeros((n_blk, qs), F32)
        for jp in range(n_blk):
            row = score[jp:jp + 1, :]
            rank = rank + jnp.where(j_s > jp, jnp.where(row >= score, 1.0, 0.0),
                                    jnp.where(row > score, 1.0, 0.0))
        sel_ref[g] = jnp.where(rank < n_top, 1.0, 0.0)

    pad_rows = jnp.zeros((LANE - HEAD_DIM - BLK_GROUP, W), BF16)

    def augmented_queries(g, group):
        rows = sel_ref[g, pl.ds(pl.multiple_of(group * BLK_GROUP, BLK_GROUP), BLK_GROUP), :]
        bias = tile_heads((rows - 1.0) * -NEG_INF).astype(BF16)
        return jnp.concatenate([q_rot[g], bias, pad_rows], axis=0)

    t_blk0 = t_step // SLC_BLOCK
    strip = 2 * qs
    n_strip = W // strip
    th = tk // 2

    def score_strips(kt, half):
        k0 = pl.multiple_of(kt * tk + half * th, th)
        keys = ks_ref[0, pl.ds(k0, th), :]
        for g in range(G):
            q_aug = augmented_queries(g, kt // (BLK_GROUP // blk_per_tile))
            for c in range(n_strip):
                s_ref[half, g * n_strip + c] = jnp.dot(keys, q_aug[:, c * strip:(c + 1) * strip],
                                                       preferred_element_type=F32)

    t_al = pl.multiple_of(t_step, tq)
    s_diag = [jnp.dot(ks_ref[0, pl.ds(t_al, (g + 1) * qs), :], augmented_queries(g, t_blk0 // BLK_GROUP),
                      preferred_element_type=F32) for g in range(G)]
    score_strips(0, 0)

    o_win = []
    for t0, w0, s in zip(t_sub, w_start, s_win):
        diff = (t0 + lax.broadcasted_iota(jnp.int32, (win_len, qs), 1)
                - w0 - lax.broadcasted_iota(jnp.int32, (win_len, qs), 0))
        s_w = s + tile_heads(jnp.where((diff >= 0) & (diff < WINDOW), 0.0, NEG_INF))
        e_w = jnp.exp2((s_w - jnp.max(s_w, axis=0, keepdims=True)).astype(BF16))
        o_w = jnp.dot(vw_ref[0, :, pl.ds(w0, win_len)], e_w, preferred_element_type=F32)
        o_win.append(o_w[:HEAD_DIM] / o_w[HEAD_DIM:HEAD_DIM + 1])

    init = []
    for g in range(G):
        n_diag = (g + 1) * qs
        causal = (lax.broadcasted_iota(jnp.int32, (n_diag, qs), 0)
                  <= g * qs + lax.broadcasted_iota(jnp.int32, (n_diag, qs), 1))
        s = s_diag[g] + tile_heads(jnp.where(causal, 0.0, NEG_INF))
        m_new = jnp.max(s, axis=0, keepdims=True)
        p = jnp.exp2((s - m_new).astype(BF16))
        acc = jnp.dot(vs_ref[0, :, pl.ds(t_al, n_diag)], p, preferred_element_type=F32)
        for c in range(n_strip):
            cols = slice(c * strip, (c + 1) * strip)
            init.append((m_new[:, cols], acc[:, cols]))

    def softmax_strips(kt, half, stats):
        k0 = pl.multiple_of(kt * tk + half * th, th)
        vals = vs_ref[0, :, pl.ds(k0, th)]
        out = []
        for c, (m_old, acc_old) in enumerate(stats):
            s = s_ref[half, c]
            m_new = jnp.maximum(m_old, jnp.max(s, axis=0, keepdims=True))
            p = jnp.exp2((s - m_new).astype(BF16))
            alpha = jnp.exp2(m_old - m_new)
            out.append((m_new, alpha * acc_old + jnp.dot(vals, p, preferred_element_type=F32)))
        return tuple(out)

    n_tiles = t_step // tk

    def key_tile(kt, stats):
        score_strips(kt, 1)
        stats = softmax_strips(kt, 0, stats)
        score_strips(jnp.minimum(kt + 1, n_tiles - 1), 0)
        return softmax_strips(kt, 1, stats)

    final = lax.fori_loop(0, n_tiles, key_tile, tuple(init))

    for g in range(G):
        gate = _sigmoid(gate_ref[g * qs:(g + 1) * qs, :].astype(F32).T)
        o_sel = jnp.concatenate([acc_fin[:HEAD_DIM] / acc_fin[HEAD_DIM:HEAD_DIM + 1]
                                 for _, acc_fin in final[g * n_strip:(g + 1) * n_strip]], axis=1)
        outs = []
        for h in range(H):
            c = h * N_BRANCH
            cols = slice(h * qs, (h + 1) * qs)
            outs.append(gate[c:c + 1, :] * o_cmp[g][:, cols] + gate[c + 1:c + 2, :] * o_sel[:, cols]
                        + gate[c + 2:c + 3, :] * o_win[g][:, cols])
        o_ref[g * qs:(g + 1) * qs, :] = jnp.concatenate(outs, axis=0).T.astype(o_ref.dtype)


def _cmp_to_slc_t(n_cmp_pad, n_blk):
    n_cmp = n_cmp_pad - 1
    cs = np.arange(n_cmp) * CMP_STRIDE
    ss = np.arange(n_blk) * SLC_BLOCK
    ov = np.minimum(cs[:, None] + CMP_BLOCK, ss[None, :] + SLC_BLOCK) - np.maximum(cs[:, None], ss[None, :])
    m = np.zeros((n_cmp_pad, n_blk), np.float32)
    m[:n_cmp] = np.clip(ov, 0, None) / CMP_STRIDE
    return jnp.asarray(m.T, dtype=BF16)


def _nsa_attention(qp, qr, kc, vc, ks, vs, kw, vw, gates, batch, seq, tq, qs, tk):
    n_blk = seq // SLC_BLOCK
    n_cmp_pad = kc.shape[1]
    n_top = min(N_SLC, n_blk)
    win_len = min(WINDOW + qs, seq)
    tpb = seq // tq
    G = tq // qs
    W = N_GROUP_HEADS * qs
    full = lambda shape: pl.BlockSpec((1,) + shape, lambda b, i: (b, 0, 0))
    q_spec = pl.BlockSpec((1, GROUP_WIDTH, tq), lambda b, i: (b, 0, i))
    return pl.pallas_call(
        functools.partial(_nsa_attn_kernel, tq=tq, qs=qs, tk=tk, seq=seq, n_top=n_top, win_len=win_len),
        out_shape=jax.ShapeDtypeStruct((batch * seq, GROUP_WIDTH), BF16),
        grid=(batch, tpb),
        in_specs=[q_spec, q_spec,
                  full((n_cmp_pad, HEAD_DIM)), full((HEAD_DIM, n_cmp_pad)),
                  full((seq, LANE)), full((V_ROWS, seq)),
                  full((seq, HEAD_DIM)), full((V_ROWS, seq)),
                  pl.BlockSpec((tq, LANE), lambda b, i: (b * tpb + i, 0)),
                  pl.BlockSpec((n_blk, n_cmp_pad), lambda b, i: (0, 0))],
        out_specs=pl.BlockSpec((tq, GROUP_WIDTH), lambda b, i: (b * tpb + i, 0)),
        scratch_shapes=[pltpu.VMEM((G, n_blk, qs), F32),
                        pltpu.VMEM((2, W * G // (2 * qs), tk // 2, 2 * qs), F32)],
        compiler_params=_cparams(("parallel", "parallel")),
        name="nsa_attn",
    )(qp, qr, kc, vc, ks, vs, kw, vw, gates, _cmp_to_slc_t(n_cmp_pad, n_blk))


def _gla_kernel(qk_ref, v_ref, g_ref, lr_ref, w2_ref, gb_ref, o_ref, state_ref, *, n_chunks):
    H, C = N_GROUP_HEADS, GLA_CHUNK
    DKW = H * GLA_DK
    DVW = H * GLA_DV

    @pl.when(pl.program_id(1) == 0)
    def _():
        state_ref[...] = jnp.zeros(state_ref.shape, F32)

    tri = (lax.broadcasted_iota(jnp.int32, (C, C), 0)
           >= lax.broadcasted_iota(jnp.int32, (C, C), 1))
    tri_b = jnp.where(tri, 1.0, 0.0).astype(BF16)
    qhead = lax.broadcasted_iota(jnp.int32, (C, DKW), 1) // GLA_DK
    vhead = lax.broadcasted_iota(jnp.int32, (C, DVW), 1) // GLA_DV
    causal = jnp.concatenate([tri] * H, axis=0)
    state_mask = (lax.broadcasted_iota(jnp.int32, (DKW, DVW), 0) // GLA_DK
                  == lax.broadcasted_iota(jnp.int32, (DKW, DVW), 1) // GLA_DV)

    pre = _dot_rhs3(lr_ref[...], w2_ref[...]) + gb_ref[...]
    log_a = (jnp.minimum(pre, 0.0) - jnp.log1p(jnp.exp(-jnp.abs(pre)))) / GLA_TAU
    q_all = qk_ref[:, 0:DKW].astype(F32) * (GLA_DK ** -0.5)
    k_all = qk_ref[:, DKW:2 * DKW].astype(F32)

    chunk_rows = [slice(ci * C, (ci + 1) * C) for ci in range(n_chunks)]
    bf = lambda a, w: jnp.dot(a, w, preferred_element_type=F32)
    la_hi, la_mid, la_lo = _split3(log_a)
    b_all = [bf(tri_b, la_hi[r]) + bf(tri_b, la_mid[r]) + bf(tri_b, la_lo[r]) for r in chunk_rows]
    v_all = [v_ref[r, :] for r in chunk_rows]
    q_stack, k_in, q_dec, k_out_t, decay = [], [], [], [], []
    for r, b in zip(chunk_rows, b_all):
        b_mid = b[C // 2:C // 2 + 1, :]
        b_last = b[C - 1:C, :]
        q, k = q_all[r], k_all[r]
        q_in = q * jnp.exp(b - b_mid)
        q_stack.append(jnp.concatenate([jnp.where(qhead == h, q_in, 0.0) for h in range(H)],
                                       axis=0).astype(BF16))
        k_in.append((k * jnp.exp(b_mid - b)).astype(BF16))
        q_dec.append((q * jnp.exp(b)).astype(BF16))
        k_out_t.append((k * jnp.exp(b_last - b)).T.astype(BF16))
        decay.append(jnp.exp(jnp.broadcast_to(b_last, (8, DKW))).T[:, 0:1])
    att = [lax.dot_general(qs_, ki_, (((1,), (1,)), ((), ())), preferred_element_type=F32)
           for qs_, ki_ in zip(q_stack, k_in)]
    att = [jnp.where(causal, a, 0.0).astype(BF16) for a in att]
    o_stack = [bf(a, v) for a, v in zip(att, v_all)]
    kv_new = [jnp.where(state_mask, bf(kt_, v), 0.0) for kt_, v in zip(k_out_t, v_all)]
    o_intra = []
    for os_ in o_stack:
        o = jnp.where(vhead == 0, os_[0:C], 0.0)
        for h in range(1, H):
            o = o + jnp.where(vhead == h, os_[h * C:(h + 1) * C], 0.0)
        o_intra.append(o)

    state = state_ref[...]
    o_chunks = []
    for ci in range(n_chunks):
        o_chunks.append(o_intra[ci] + bf(q_dec[ci], state.astype(BF16)))
        state = decay[ci] * state + kv_new[ci]
    state_ref[...] = state

    o = jnp.concatenate(o_chunks, axis=0)
    ms = _dot_lhs3(o * o, _group_mean_matrix(DVW, GLA_DV))
    gate = g_ref[...].astype(F32)
    o_ref[...] = (gate * _sigmoid(gate) * (o * lax.rsqrt(ms + RMS_EPS))).astype(o_ref.dtype)


def _gla(qk, v, g, lr, gate_w2_l, gate_b_l, batch, seq, tg):
    tpb = seq // tg
    row = lambda b, i: (b * tpb + i, 0)
    w2 = jnp.zeros((LANE, N_GROUP_HEADS * GLA_DK), F32).at[N_GATE_COLS:N_GATE_COLS + GLA_RANK].set(gate_w2_l)
    return pl.pallas_call(
        functools.partial(_gla_kernel, n_chunks=tg // GLA_CHUNK),
        out_shape=jax.ShapeDtypeStruct((batch * seq, GROUP_WIDTH), BF16),
        grid=(batch, tpb),
        in_specs=[pl.BlockSpec((tg, 2 * N_GROUP_HEADS * GLA_DK), row),
                  pl.BlockSpec((tg, GROUP_WIDTH), row),
                  pl.BlockSpec((tg, GROUP_WIDTH), row),
                  pl.BlockSpec((tg, LANE), row),
                  pl.BlockSpec((LANE, N_GROUP_HEADS * GLA_DK), lambda b, i: (0, 0)),
                  pl.BlockSpec((1, N_GROUP_HEADS * GLA_DK), lambda b, i: (0, 0))],
        out_specs=pl.BlockSpec((tg, GROUP_WIDTH), row),
        scratch_shapes=[pltpu.VMEM((N_GROUP_HEADS * GLA_DK, N_GROUP_HEADS * GLA_DV), F32)],
        compiler_params=_cparams(("parallel", "arbitrary")),
        name="gla",
    )(qk, v, g, lr, w2, gate_b_l.reshape(1, -1))


def _gmlp_kernel(uv_ref, w_ref, bias_ref, o_ref, *, n_chunks):
    T, G = GMLP_CHUNK, N_GROUP_HEADS
    gdim = GROUP_WIDTH // G
    gmean = _group_mean_matrix(GROUP_WIDTH, gdim)
    tri = (lax.broadcasted_iota(jnp.int32, (T, T), 0)
           >= lax.broadcasted_iota(jnp.int32, (T, T), 1))
    head = lax.broadcasted_iota(jnp.int32, (T, GROUP_WIDTH), 1) // gdim
    w = [jnp.where(tri, w_ref[g], 0.0).astype(BF16) for g in range(G)]
    v = uv_ref[:, GROUP_WIDTH:2 * GROUP_WIDTH]
    d = v.astype(F32) - jnp.dot(v, gmean, preferred_element_type=F32)
    vn = (d * lax.rsqrt(_dot_lhs3(d * d, gmean) + RMS_EPS)).astype(BF16)
    mixes = [[jnp.dot(w[g], vn[ci * T:(ci + 1) * T], preferred_element_type=F32) for g in range(G)]
             for ci in range(n_chunks)]
    for ci in range(n_chunks):
        rows = slice(ci * T, (ci + 1) * T)
        mixed = bias_ref[...]
        for g in range(G):
            mixed = mixed + jnp.where(head == g, mixes[ci][g], 0.0)
        o_ref[rows, :] = (uv_ref[rows, 0:GROUP_WIDTH].astype(F32) * mixed).astype(o_ref.dtype)


def _gmlp(uv, gmlp_ws_l, gmlp_b_l, rows, tc):
    gdim = GROUP_WIDTH // N_GROUP_HEADS
    bias = jnp.repeat(gmlp_b_l.T, gdim, axis=1)
    return pl.pallas_call(
        functools.partial(_gmlp_kernel, n_chunks=tc // GMLP_CHUNK),
        out_shape=jax.ShapeDtypeStruct((rows, GROUP_WIDTH), BF16),
        grid=(rows // tc,),
        in_specs=[pl.BlockSpec((tc, 2 * GROUP_WIDTH), lambda i: (i, 0)),
                  pl.BlockSpec((N_GROUP_HEADS, GMLP_CHUNK, GMLP_CHUNK), lambda i: (0, 0, 0)),
                  pl.BlockSpec((GMLP_CHUNK, GROUP_WIDTH), lambda i: (0, 0))],
        out_specs=pl.BlockSpec((tc, GROUP_WIDTH), lambda i: (i, 0)),
        compiler_params=_cparams(("parallel",)),
        name="gmlp",
    )(uv, gmlp_ws_l, bias)


def _tail_kernel(conv_ref, halo_ref, cw_ref, nsa_ref, gla_ref, gmlp_ref, gain_ref, wo_ref,
                 x_ref, mod_ref, g_ref, wu_ref, wd_ref, o_ref, *, tpb, tf):
    GW = GROUP_WIDTH
    tm = conv_ref.shape[0]
    nh = halo_ref.shape[0]
    z = conv_ref[:, 2 * GW:3 * GW].astype(F32) * conv_ref[:, 0:GW].astype(F32)
    zh = halo_ref[:, 2 * GW:3 * GW].astype(F32) * halo_ref[:, 0:GW].astype(F32)
    zh = jnp.where(pl.program_id(0) % tpb == 0, 0.0, zh)
    row = lax.broadcasted_iota(jnp.int32, (tm, GW), 0)
    y = cw_ref[CONV_WIDTH - 1:CONV_WIDTH, :] * z
    for back in range(1, CONV_WIDTH):
        shifted = pltpu.roll(z, back, axis=0)
        for r in range(back):
            shifted = jnp.where(row == r, zh[nh - back + r:nh - back + r + 1, :], shifted)
        y = y + cw_ref[CONV_WIDTH - 1 - back:CONV_WIDTH - back, :] * shifted
    conv_out = conv_ref[:, GW:2 * GW].astype(F32) * y

    halves = [slice(0, tm // 2), slice(tm // 2, tm)]
    x_mid = []
    for r in halves:
        groups = (conv_out[r], nsa_ref[r, :], gla_ref[r, :], gmlp_ref[r, :])
        acc = jnp.zeros((tm // 2, D_MODEL), F32)
        for k, o in enumerate(groups):
            n = (_rms(o.astype(F32)) * gain_ref[k:k + 1, :]).astype(BF16)
            acc = acc + jnp.dot(n, wo_ref[0, k * GW:(k + 1) * GW, :], preferred_element_type=F32)
        x_mid.append(x_ref[r, :] + mod_ref[0, 2:3, :] * (_rms(acc) * g_ref[1:2, :]))

    hb = [((_rms(x) * g_ref[2:3, :]) * (1.0 + mod_ref[0, 4:5, :]) + mod_ref[0, 3:4, :]).astype(BF16)
          for x in x_mid]
    acc = [jnp.zeros((tm // 2, D_MODEL), F32) for _ in halves]
    for f in range(D_FF // tf):
        up = [jnp.dot(h, wu_ref[0, :, f * tf:(f + 1) * tf], preferred_element_type=F32) for h in hb]
        for i, u in enumerate(up):
            u = jnp.maximum(u, 0.0)
            acc[i] = acc[i] + jnp.dot((u * u).astype(BF16), wd_ref[0, f * tf:(f + 1) * tf, :],
                                      preferred_element_type=F32)
    for r, x, a in zip(halves, x_mid, acc):
        o_ref[r, :] = x + mod_ref[0, 5:6, :] * (_rms(a) * g_ref[3:4, :])


def _layer_tail(conv, nsa, gla, gmlp, conv_w_l, grp_gain_l, wo_all, wu_all, wd_all, layer,
                x2, mod_l, norm_g_l, seq, tm):
    rows = x2.shape[0]
    tpb = seq // tm
    halo = 16
    hpt = tm // halo
    gw = lambda i: (i, 0)
    const = lambda i: (0, 0)
    weight = lambda shape: pl.BlockSpec((1,) + shape, lambda i: (layer, 0, 0),
                                        pipeline_mode=pl.Buffered(1))
    return pl.pallas_call(
        functools.partial(_tail_kernel, tpb=tpb, tf=D_MODEL),
        out_shape=jax.ShapeDtypeStruct((rows, D_MODEL), F32),
        grid=(rows // tm,),
        in_specs=[pl.BlockSpec((tm, 3 * GROUP_WIDTH), gw),
                  pl.BlockSpec((halo, 3 * GROUP_WIDTH), lambda i: (jnp.maximum(i * hpt - 1, 0), 0)),
                  pl.BlockSpec((CONV_WIDTH, GROUP_WIDTH), const),
                  pl.BlockSpec((tm, GROUP_WIDTH), gw),
                  pl.BlockSpec((tm, GROUP_WIDTH), gw),
                  pl.BlockSpec((tm, GROUP_WIDTH), gw),
                  pl.BlockSpec((N_MIXERS, GROUP_WIDTH), const),
                  weight((D_MODEL, D_MODEL)),
                  pl.BlockSpec((tm, D_MODEL), gw),
                  pl.BlockSpec((1, N_MOD, D_MODEL), lambda i: (i // tpb, 0, 0)),
                  pl.BlockSpec((4, D_MODEL), const),
                  weight((D_MODEL, D_FF)),
                  weight((D_FF, D_MODEL))],
        out_specs=pl.BlockSpec((tm, D_MODEL), gw),
        compiler_params=_cparams(("parallel",)),
        name="layer_tail",
    )(conv, conv, conv_w_l, nsa, gla, gmlp, grp_gain_l.reshape(N_MIXERS, GROUP_WIDTH), wo_all,
      x2, mod_l, norm_g_l, wu_all, wd_all)


def _pack_in_proj(w_in):
    offs = np.cumsum((0,) + (3 * GROUP_WIDTH, GROUP_WIDTH, 2 * N_BRANCH * HEAD_DIM, N_GATE_COLS,
                             2 * N_GROUP_HEADS * GLA_DK, GROUP_WIDTH, GROUP_WIDTH, GLA_RANK,
                             2 * GROUP_WIDTH))
    col = lambda a, b: w_in[..., offs[a]:offs[b]]
    narrow_pad = jnp.zeros(w_in.shape[:-1] + (LANE - N_GATE_COLS - GLA_RANK,), w_in.dtype)
    return jnp.concatenate([col(0, 3),
                            col(3, 4), col(7, 8), narrow_pad,
                            col(4, 7),
                            col(8, 9)], axis=-1).astype(BF16)


def kernel(x, c, positions, w_in, conv_w, cmp_pos, cmp_w1, cmp_w2, gla_gate_w2, gla_gate_b, gmlp_ws, gmlp_b, grp_gain, w_o, norm_g, w_mod, b_mod, w_up, w_down):
    batch, seq, _ = x.shape
    depth = w_in.shape[0]
    rows = batch * seq
    tm = min(512, seq)
    qs = min(128, seq)
    tq = min(4 * qs, seq)
    tk = min(512, seq)

    mod = _modulation(c, w_mod, b_mod).reshape(depth, batch, N_MOD, D_MODEL)
    cos_t, sin_t = _rope_tables(positions)
    x2 = x.reshape(rows, D_MODEL)
    w_in_b, w_o_b, w_up_b, w_down_b = _pack_in_proj(w_in), w_o.astype(BF16), w_up.astype(BF16), w_down.astype(BF16)
    for l in range(depth):
        conv, n_q, n_kv, narrow, l_qk, l_v, l_g, m_uv = _in_projection(
            x2, mod[l], norm_g[l], w_in_b, l, seq, tm)
        qp, qr, ks, vs, kw, vw = _nsa_prep(n_q, n_kv, cos_t, sin_t, batch, seq, tm)
        kc, vc = _nsa_compress(n_kv, cmp_pos[l], cmp_w1[l], cmp_w2[l], batch, seq)
        nsa = _nsa_attention(qp, qr, kc, vc, ks, vs, kw, vw, narrow, batch, seq, tq, qs, tk)
        gla = _gla(l_qk, l_v, l_g, narrow, gla_gate_w2[l], gla_gate_b[l], batch, seq, tm)
        gmlp = _gmlp(m_uv, gmlp_ws[l], gmlp_b[l], rows, tm)
        x2 = _layer_tail(conv, nsa, gla, gmlp, conv_w[l], grp_gain[l], w_o_b, w_up_b, w_down_b, l,
                         x2, mod[l], norm_g[l], seq, tm)
    return x2.reshape(batch, seq, D_MODEL)
```

```python
import functools

import numpy as np
import jax
import jax.numpy as jnp
from jax import lax
from jax.experimental import pallas as pl
from jax.experimental.pallas import tpu as pltpu

D_MODEL = 1024
N_MIXERS = 4
GROUP_WIDTH = D_MODEL // N_MIXERS
HEAD_DIM = 64
N_GROUP_HEADS = GROUP_WIDTH // HEAD_DIM
D_FF = 4 * D_MODEL
N_MOD = 6
RMS_EPS = 1e-6
NEG_INF = -1e30

ROPE_THETA = 500000.0
ROPE_DIM = HEAD_DIM // 4
ROPE_HALF = ROPE_DIM // 2

CONV_WIDTH = 3

CMP_BLOCK = 32
CMP_STRIDE = 16
CMP_HIDDEN = 2 * HEAD_DIM
SLC_BLOCK = 64
N_SLC = 16
N_INIT_BLOCKS = 1
N_LOCAL_BLOCKS = 2
FORCE_BONUS = 1e4
WINDOW = 512
N_BRANCH = 3
BLK_GROUP = 16
LOG2E = 1.4426950408889634
V_ROWS = HEAD_DIM + 16

GLA_DK = HEAD_DIM // 2
GLA_DV = HEAD_DIM
GLA_RANK = 16
GLA_TAU = 16.0
GLA_CHUNK = 64

GMLP_CHUNK = 128

LANE = 128
VMEM_LIMIT = 48 * 1024 * 1024

F32 = jnp.float32
BF16 = jnp.bfloat16

N_GATE_COLS = N_BRANCH * N_GROUP_HEADS
SEG_WIDTHS = (3 * GROUP_WIDTH,
              GROUP_WIDTH,
              2 * N_BRANCH * HEAD_DIM,
              LANE,
              2 * N_GROUP_HEADS * GLA_DK,
              GROUP_WIDTH,
              GROUP_WIDTH,
              2 * GROUP_WIDTH)
SEG_DTYPES = (BF16, BF16, F32, BF16, BF16, BF16, BF16, BF16)
SEG_OFFS = tuple(int(v) for v in np.cumsum((0,) + SEG_WIDTHS))
D_IN_PAD = SEG_OFFS[-1]
DOT_GROUPS = ((0,), (1,), (2, 3), (4,), (5,), (6,), (7,))


def _cparams(sem):
    return pltpu.CompilerParams(dimension_semantics=sem, vmem_limit_bytes=VMEM_LIMIT)


def _bdot(a, b):
    return jnp.dot(a.astype(BF16), b.astype(BF16), preferred_element_type=F32)


def _dot_nt(a, b):
    return lax.dot_general(a.astype(BF16), b.astype(BF16), (((1,), (1,)), ((), ())),
                           preferred_element_type=F32)


def _split2(a):
    hi = a.astype(BF16)
    lo = (a - hi.astype(F32)).astype(BF16)
    return hi, lo


def _split3(a):
    hi = a.astype(BF16)
    r = a - hi.astype(F32)
    mid = r.astype(BF16)
    lo = (r - mid.astype(F32)).astype(BF16)
    return hi, mid, lo


def _dot_lhs3(a, b_exact):
    hi, mid, lo = _split3(a)
    b = b_exact.astype(BF16)
    d = lambda p: jnp.dot(p, b, preferred_element_type=F32)
    return d(hi) + d(mid) + d(lo)


def _dot_rhs3(a_exact, b):
    hi, mid, lo = _split3(b)
    a = a_exact.astype(BF16)
    d = lambda p: jnp.dot(a, p, preferred_element_type=F32)
    return d(hi) + d(mid) + d(lo)


def _dot_f32(a, b):
    ah, al = _split2(a)
    bh, bl = _split2(b)
    d = lambda p, q: jnp.dot(p, q, preferred_element_type=F32)
    return d(ah, bh) + d(al, bh) + d(ah, bl)


def _rms(x):
    return x * lax.rsqrt(jnp.mean(x * x, axis=-1, keepdims=True) + RMS_EPS)


def _sigmoid(x):
    return 1.0 / (1.0 + jnp.exp(-x))


def _group_mean_matrix(width, group):
    r = lax.broadcasted_iota(jnp.int32, (width, width), 0) // group
    c = lax.broadcasted_iota(jnp.int32, (width, width), 1) // group
    return jnp.where(r == c, 1.0 / group, 0.0).astype(BF16)


def _mod_kernel(c_ref, w_ref, b_ref, o_ref):
    c = c_ref[...]
    cond = c * _sigmoid(c)
    o_ref[0] = _dot_f32(cond, w_ref[0]) + b_ref[0]


def _modulation(c, w_mod, b_mod):
    depth, d, n = w_mod.shape
    b = c.shape[0]
    tn = D_MODEL
    return pl.pallas_call(
        _mod_kernel,
        out_shape=jax.ShapeDtypeStruct((depth, b, n), F32),
        grid=(depth, n // tn),
        in_specs=[pl.BlockSpec((b, d), lambda l, j: (0, 0)),
                  pl.BlockSpec((1, d, tn), lambda l, j: (l, 0, j)),
                  pl.BlockSpec((1, 1, tn), lambda l, j: (l, 0, j))],
        out_specs=pl.BlockSpec((1, b, tn), lambda l, j: (l, 0, j)),
        compiler_params=_cparams(("parallel", "parallel")),
        name="adaln_mod",
    )(c, w_mod, b_mod.reshape(depth, 1, n))


def _rope_table_kernel(pos_ref, freq_ref, sign_ref, cos_ref, sin_ref):
    ang = pos_ref[...].astype(F32) * freq_ref[...]
    cos_ref[...] = jnp.cos(ang)
    sin_ref[...] = jnp.sin(ang) * sign_ref[...]


def _rope_tables(positions):
    b, s = positions.shape
    rows = b * s
    tr = min(1024, rows)
    inv_freq = ROPE_THETA ** (-jnp.arange(0, ROPE_DIM, 2, dtype=F32) / ROPE_DIM)
    lane = np.arange(LANE) % HEAD_DIM
    freq = jnp.where(lane < ROPE_DIM, inv_freq[lane % ROPE_HALF], 0.0).astype(F32)
    sign = jnp.asarray(np.where(lane < ROPE_HALF, -1.0, 1.0), dtype=F32)
    shape = jax.ShapeDtypeStruct((rows, LANE), F32)
    return pl.pallas_call(
        _rope_table_kernel,
        out_shape=(shape, shape),
        grid=(rows // tr,),
        in_specs=[pl.BlockSpec((tr, 1), lambda i: (i, 0)),
                  pl.BlockSpec((1, LANE), lambda i: (0, 0)),
                  pl.BlockSpec((1, LANE), lambda i: (0, 0))],
        out_specs=(pl.BlockSpec((tr, LANE), lambda i: (i, 0)),
                   pl.BlockSpec((tr, LANE), lambda i: (i, 0))),
        compiler_params=_cparams(("parallel",)),
        name="rope_tables",
    )(positions.reshape(rows, 1), freq.reshape(1, -1), sign.reshape(1, -1))


def _inproj_kernel(x_ref, mod_ref, g_ref, w_ref, *out_refs):
    h = _rms(x_ref[...]) * g_ref[0:1, :]
    hb = (h * (1.0 + mod_ref[0, 1:2, :]) + mod_ref[0, 0:1, :]).astype(BF16)
    for group in DOT_GROUPS:
        lo, hi = SEG_OFFS[group[0]], SEG_OFFS[group[-1] + 1]
        z = jnp.dot(hb, w_ref[0, :, lo:hi], preferred_element_type=F32)
        for k in group:
            o_ref = out_refs[k]
            o_ref[...] = z[:, SEG_OFFS[k] - lo:SEG_OFFS[k + 1] - lo].astype(o_ref.dtype)


def _in_projection(x2, mod_l, norm_g_l, w_packed, layer, seq, tm):
    rows = x2.shape[0]
    tpb = seq // tm
    outs = tuple(jax.ShapeDtypeStruct((rows, w), dt) for w, dt in zip(SEG_WIDTHS, SEG_DTYPES))
    return pl.pallas_call(
        _inproj_kernel,
        out_shape=outs,
        grid=(rows // tm,),
        in_specs=[pl.BlockSpec((tm, D_MODEL), lambda i: (i, 0)),
                  pl.BlockSpec((1, N_MOD, D_MODEL), lambda i: (i // tpb, 0, 0)),
                  pl.BlockSpec((4, D_MODEL), lambda i: (0, 0)),
                  pl.BlockSpec((1, D_MODEL, D_IN_PAD), lambda i: (layer, 0, 0))],
        out_specs=tuple(pl.BlockSpec((tm, w), lambda i: (i, 0)) for w in SEG_WIDTHS),
        compiler_params=_cparams(("parallel",)),
        name="in_proj",
    )(x2, mod_l, norm_g_l, w_packed)


def _rope(x, cos, sin):
    w = x.shape[-1]
    lane = lax.broadcasted_iota(jnp.int32, x.shape, 1) % HEAD_DIM
    swapped = jnp.where(lane < ROPE_HALF,
                        pltpu.roll(x, w - ROPE_HALF, axis=1),
                        pltpu.roll(x, ROPE_HALF, axis=1))
    return x * cos + swapped * sin


def _nsa_prep_kernel(q_ref, kv_ref, cos_ref, sin_ref,
                     qp_ref, qr_ref, ks_ref, vs_ref, kw_ref, vw_ref):
    tp = cos_ref.shape[0]
    cos = cos_ref[...]
    sin = sin_ref[...]
    q = q_ref[...].astype(F32) * (HEAD_DIM ** -0.5 * LOG2E)
    qp_ref[0] = q.T.astype(BF16)
    reps = GROUP_WIDTH // LANE
    qr_ref[0] = _rope(q, jnp.concatenate([cos] * reps, axis=1),
                      jnp.concatenate([sin] * reps, axis=1)).T.astype(BF16)
    lane = lax.broadcasted_iota(jnp.int32, (tp, LANE), 1)
    is_k = lane < HEAD_DIM
    cos_kv = jnp.where(is_k, cos, 1.0)
    sin_kv = jnp.where(is_k, sin, 0.0)
    slc = _rope(kv_ref[:, LANE:2 * LANE], cos_kv, sin_kv)
    win = _rope(kv_ref[:, 2 * LANE:3 * LANE], cos_kv, sin_kv)
    blk = ((pl.program_id(1) * tp + lax.broadcasted_iota(jnp.int32, (tp, LANE), 0)) // SLC_BLOCK) % BLK_GROUP
    onehot = jnp.where(lane - HEAD_DIM == blk, 1.0, 0.0)
    ks_ref[0] = jnp.where(is_k, slc, onehot).astype(BF16)
    ones_rows = jnp.where(lax.broadcasted_iota(jnp.int32, (V_ROWS - HEAD_DIM, tp), 0) == 0, 1.0, 0.0)
    vs_ref[0] = jnp.concatenate([slc.T[HEAD_DIM:, :], ones_rows], axis=0).astype(BF16)
    kw_ref[0] = win[:, :HEAD_DIM].astype(BF16)
    vw_ref[0] = jnp.concatenate([win.T[HEAD_DIM:, :], ones_rows], axis=0).astype(BF16)


def _nsa_prep(q, kv, cos_t, sin_t, batch, seq, tp):
    tpb = seq // tp
    row = lambda b, i: (b * tpb + i, 0)
    qs = jax.ShapeDtypeStruct((batch, GROUP_WIDTH, seq), BF16)
    ks = jax.ShapeDtypeStruct((batch, seq, LANE), BF16)
    kw = jax.ShapeDtypeStruct((batch, seq, HEAD_DIM), BF16)
    vs = jax.ShapeDtypeStruct((batch, V_ROWS, seq), BF16)
    q_spec = pl.BlockSpec((1, GROUP_WIDTH, tp), lambda b, i: (b, 0, i))
    ks_spec = pl.BlockSpec((1, tp, LANE), lambda b, i: (b, i, 0))
    kw_spec = pl.BlockSpec((1, tp, HEAD_DIM), lambda b, i: (b, i, 0))
    v_spec = pl.BlockSpec((1, V_ROWS, tp), lambda b, i: (b, 0, i))
    return pl.pallas_call(
        _nsa_prep_kernel,
        out_shape=(qs, qs, ks, vs, kw, vs),
        grid=(batch, tpb),
        in_specs=[pl.BlockSpec((tp, GROUP_WIDTH), row),
                  pl.BlockSpec((tp, 2 * N_BRANCH * HEAD_DIM), row),
                  pl.BlockSpec((tp, LANE), row),
                  pl.BlockSpec((tp, LANE), row)],
        out_specs=(q_spec, q_spec, ks_spec, v_spec, kw_spec, v_spec),
        compiler_params=_cparams(("parallel", "parallel")),
        name="nsa_prep",
    )(q, kv, cos_t, sin_t)


def _nsa_compress_kernel(kv_ref, pos_ref, w1_ref, w2_ref, kc_ref, vc_ref, *, n_half):
    half = CMP_BLOCK // 2
    assert half == CMP_STRIDE
    chunks = [kv_ref[pl.ds(r, n_half, stride=CMP_STRIDE), :] for r in range(half)]
    outs = []
    for j in range(2):
        first = jnp.zeros((n_half, CMP_HIDDEN), F32)
        second = jnp.zeros((n_half, CMP_HIDDEN), F32)
        for r in range(half):
            t = chunks[r][:, j * HEAD_DIM:(j + 1) * HEAD_DIM]
            first += _bdot(t + pos_ref[j, r:r + 1, :],
                           w1_ref[j, r * HEAD_DIM:(r + 1) * HEAD_DIM, :])
            second += _bdot(t + pos_ref[j, half + r:half + r + 1, :],
                            w1_ref[j, (half + r) * HEAD_DIM:(half + r + 1) * HEAD_DIM, :])
        pre = first + pltpu.roll(second, n_half - 1, axis=0)
        row = lax.broadcasted_iota(jnp.int32, pre.shape, 0)
        pre = jnp.where(row < n_half - 1, pre, 0.0)
        hid = pre * _sigmoid(pre)
        outs.append(_bdot(hid, w2_ref[j]))
    kc_ref[0] = outs[0].astype(BF16)
    vc_ref[0] = jnp.concatenate(outs, axis=1).T[HEAD_DIM:, :].astype(BF16)


def _nsa_compress(kv, cmp_pos_l, cmp_w1_l, cmp_w2_l, batch, seq):
    n_half = seq // CMP_STRIDE
    return pl.pallas_call(
        functools.partial(_nsa_compress_kernel, n_half=n_half),
        out_shape=(jax.ShapeDtypeStruct((batch, n_half, HEAD_DIM), BF16),
                   jax.ShapeDtypeStruct((batch, HEAD_DIM, n_half), BF16)),
        grid=(batch,),
        in_specs=[pl.BlockSpec((seq, LANE), lambda b: (b, 0)),
                  pl.BlockSpec((2, CMP_BLOCK, HEAD_DIM), lambda b: (0, 0, 0)),
                  pl.BlockSpec((2, CMP_BLOCK * HEAD_DIM, CMP_HIDDEN), lambda b: (0, 0, 0)),
                  pl.BlockSpec((2, CMP_HIDDEN, HEAD_DIM), lambda b: (0, 0, 0))],
        out_specs=(pl.BlockSpec((1, n_half, HEAD_DIM), lambda b: (b, 0, 0)),
                   pl.BlockSpec((1, HEAD_DIM, n_half), lambda b: (b, 0, 0))),
        compiler_params=_cparams(("parallel",)),
        name="nsa_compress",
    )(kv, cmp_pos_l, cmp_w1_l, cmp_w2_l)


def _nsa_attn_kernel(qp_ref, qr_ref, kc_ref, vc_ref, ks_ref, vs_ref, kw_ref, vw_ref, gate_ref,
                     ovl_ref, o_ref, sel_ref, s_ref,
                     *, tq, qs, tk, seq, n_top, win_len):
    H = N_GROUP_HEADS
    W = H * qs
    G = tq // qs
    n_cmp_pad = kc_ref.shape[1]
    n_cmp = n_cmp_pad - 1
    n_blk = seq // SLC_BLOCK
    blk_per_tile = tk // SLC_BLOCK
    t_step = pl.program_id(1) * tq

    def heads_on_lanes(ref, g):
        return jnp.concatenate([ref[0, h * HEAD_DIM:(h + 1) * HEAD_DIM, g * qs:(g + 1) * qs]
                                for h in range(H)], axis=1)

    def tile_heads(a):
        return jnp.concatenate([a] * H, axis=1)

    t_sub = [t_step + g * qs for g in range(G)]
    q_rot = [heads_on_lanes(qr_ref, g) for g in range(G)]
    w_start = [pl.multiple_of(jnp.maximum(t0 + qs - win_len, 0), qs) for t0 in t_sub]
    s_cmp = [jnp.dot(kc_ref[0], heads_on_lanes(qp_ref, g), preferred_element_type=F32) for g in range(G)]
    s_win = [jnp.dot(kw_ref[0, pl.ds(w0, win_len), :], q, preferred_element_type=F32)
             for w0, q in zip(w_start, q_rot)]

    p_cmp = []
    for t0, s in zip(t_sub, s_cmp):
        n_c = lax.broadcasted_iota(jnp.int32, (n_cmp_pad, qs), 0)
        t_c = t0 + lax.broadcasted_iota(jnp.int32, (n_cmp_pad, qs), 1)
        bias_c = jnp.where((n_c * CMP_STRIDE + CMP_BLOCK - 1 <= t_c) & (n_c < n_cmp), 0.0, NEG_INF)
        s_c = s + tile_heads(bias_c)
        e_c = jnp.exp2(s_c - jnp.max(s_c, axis=0, keepdims=True))
        any_valid = jnp.where(t_c[0:1, :] >= CMP_BLOCK - 1, 1.0, 0.0)
        p_cmp.append(e_c * (tile_heads(any_valid) / jnp.sum(e_c, axis=0, keepdims=True)))
    o_cmp = [jnp.dot(vc_ref[0], p_c.astype(BF16), preferred_element_type=F32) for p_c in p_cmp]

    p_slc = []
    for p_c in p_cmp:
        p_sum = p_c[:, 0:qs]
        for h in range(1, H):
            p_sum = p_sum + p_c[:, h * qs:(h + 1) * qs]
        ps_hi, ps_lo = _split2(p_sum)
        p_slc.append(jnp.dot(ovl_ref[...], ps_hi, preferred_element_type=F32)
                     + jnp.dot(ovl_ref[...], ps_lo, preferred_element_type=F32))

    for g, (t0, imp) in enumerate(zip(t_sub, p_slc)):
        j_s = lax.broadcasted_iota(jnp.int32, (n_blk, qs), 0)
        cur = (t0 + lax.broadcasted_iota(jnp.int32, (n_blk, qs), 1)) // SLC_BLOCK
        forced = (j_s < N_INIT_BLOCKS) | ((cur - j_s >= 0) & (cur - j_s < N_LOCAL_BLOCKS))
        score = jnp.where(j_s <= cur, imp + jnp.where(forced, FORCE_BONUS, 0.0), NEG_INF)
        rank = jnp.zeros((n_blk, qs), F32)
        for jp in range(n_blk):
            row = score[jp:jp + 1, :]
            rank = rank + jnp.where(j_s > jp, jnp.where(row >= score, 1.0, 0.0),
                                    jnp.where(row > score, 1.0, 0.0))
        sel_ref[g] = jnp.where(rank < n_top, 1.0, 0.0)

    pad_rows = jnp.zeros((LANE - HEAD_DIM - BLK_GROUP, W), BF16)

    def augmented_queries(g, group):
        rows = sel_ref[g, pl.ds(pl.multiple_of(group * BLK_GROUP, BLK_GROUP), BLK_GROUP), :]
        bias = tile_heads((rows - 1.0) * -NEG_INF).astype(BF16)
        return jnp.concatenate([q_rot[g], bias, pad_rows], axis=0)

    t_blk0 = t_step // SLC_BLOCK
    strip = 2 * qs
    n_strip = W // strip
    th = tk // 2

    def score_strips(kt, half):
        k0 = pl.multiple_of(kt * tk + half * th, th)
        keys = ks_ref[0, pl.ds(k0, th), :]
        for g in range(G):
            q_aug = augmented_queries(g, kt // (BLK_GROUP // blk_per_tile))
            for c in range(n_strip):
                s_ref[half, g * n_strip + c] = jnp.dot(keys, q_aug[:, c * strip:(c + 1) * strip],
                                                       preferred_element_type=F32)

    t_al = pl.multiple_of(t_step, tq)
    s_diag = [jnp.dot(ks_ref[0, pl.ds(t_al, (g + 1) * qs), :], augmented_queries(g, t_blk0 // BLK_GROUP),
                      preferred_element_type=F32) for g in range(G)]
    score_strips(0, 0)

    o_win = []
    for t0, w0, s in zip(t_sub, w_start, s_win):
        diff = (t0 + lax.broadcasted_iota(jnp.int32, (win_len, qs), 1)
                - w0 - lax.broadcasted_iota(jnp.int32, (win_len, qs), 0))
        s_w = s + tile_heads(jnp.where((diff >= 0) & (diff < WINDOW), 0.0, NEG_INF))
        e_w = jnp.exp2((s_w - jnp.max(s_w, axis=0, keepdims=True)).astype(BF16))
        o_w = jnp.dot(vw_ref[0, :, pl.ds(w0, win_len)], e_w, preferred_element_type=F32)
        o_win.append(o_w[:HEAD_DIM] / o_w[HEAD_DIM:HEAD_DIM + 1])

    init = []
    for g in range(G):
        n_diag = (g + 1) * qs
        causal = (lax.broadcasted_iota(jnp.int32, (n_diag, qs), 0)
                  <= g * qs + lax.broadcasted_iota(jnp.int32, (n_diag, qs), 1))
        s = s_diag[g] + tile_heads(jnp.where(causal, 0.0, NEG_INF))
        m_new = jnp.max(s, axis=0, keepdims=True)
        p = jnp.exp2((s - m_new).astype(BF16))
        acc = jnp.dot(vs_ref[0, :, pl.ds(t_al, n_diag)], p, preferred_element_type=F32)
        for c in range(n_strip):
            cols = slice(c * strip, (c + 1) * strip)
            init.append((m_new[:, cols], acc[:, cols]))

    def softmax_strips(kt, half, stats):
        k0 = pl.multiple_of(kt * tk + half * th, th)
        vals = vs_ref[0, :, pl.ds(k0, th)]
        out = []
        for c, (m_old, acc_old) in enumerate(stats):
            s = s_ref[half, c]
            m_new = jnp.maximum(m_old, jnp.max(s, axis=0, keepdims=True))
            p = jnp.exp2((s - m_new).astype(BF16))
            alpha = jnp.exp2(m_old - m_new)
            out.append((m_new, alpha * acc_old + jnp.dot(vals, p, preferred_element_type=F32)))
        return tuple(out)

    n_tiles = t_step // tk

    def key_tile(kt, stats):
        score_strips(kt, 1)
        stats = softmax_strips(kt, 0, stats)
        score_strips(jnp.minimum(kt + 1, n_tiles - 1), 0)
        return softmax_strips(kt, 1, stats)

    final = lax.fori_loop(0, n_tiles, key_tile, tuple(init))

    for g in range(G):
        gate = _sigmoid(gate_ref[g * qs:(g + 1) * qs, :].astype(F32).T)
        o_sel = jnp.concatenate([acc_fin[:HEAD_DIM] / acc_fin[HEAD_DIM:HEAD_DIM + 1]
                                 for _, acc_fin in final[g * n_strip:(g + 1) * n_strip]], axis=1)
        outs = []
        for h in range(H):
            c = h * N_BRANCH
            cols = slice(h * qs, (h + 1) * qs)
            outs.append(gate[c:c + 1, :] * o_cmp[g][:, cols] + gate[c + 1:c + 2, :] * o_sel[:, cols]
                        + gate[c + 2:c + 3, :] * o_win[g][:, cols])
        o_ref[g * qs:(g + 1) * qs, :] = jnp.concatenate(outs, axis=0).T.astype(o_ref.dtype)


def _cmp_to_slc_t(n_cmp_pad, n_blk):
    n_cmp = n_cmp_pad - 1
    cs = np.arange(n_cmp) * CMP_STRIDE
    ss = np.arange(n_blk) * SLC_BLOCK
    ov = np.minimum(cs[:, None] + CMP_BLOCK, ss[None, :] + SLC_BLOCK) - np.maximum(cs[:, None], ss[None, :])
    m = np.zeros((n_cmp_pad, n_blk), np.float32)
    m[:n_cmp] = np.clip(ov, 0, None) / CMP_STRIDE
    return jnp.asarray(m.T, dtype=BF16)


def _nsa_attention(qp, qr, kc, vc, ks, vs, kw, vw, gates, batch, seq, tq, qs, tk):
    n_blk = seq // SLC_BLOCK
    n_cmp_pad = kc.shape[1]
    n_top = min(N_SLC, n_blk)
    win_len = min(WINDOW + qs, seq)
    tpb = seq // tq
    G = tq // qs
    W = N_GROUP_HEADS * qs
    full = lambda shape: pl.BlockSpec((1,) + shape, lambda b, i: (b, 0, 0))
    q_spec = pl.BlockSpec((1, GROUP_WIDTH, tq), lambda b, i: (b, 0, i))
    return pl.pallas_call(
        functools.partial(_nsa_attn_kernel, tq=tq, qs=qs, tk=tk, seq=seq, n_top=n_top, win_len=win_len),
        out_shape=jax.ShapeDtypeStruct((batch * seq, GROUP_WIDTH), BF16),
        grid=(batch, tpb),
        in_specs=[q_spec, q_spec,
                  full((n_cmp_pad, HEAD_DIM)), full((HEAD_DIM, n_cmp_pad)),
                  full((seq, LANE)), full((V_ROWS, seq)),
                  full((seq, HEAD_DIM)), full((V_ROWS, seq)),
                  pl.BlockSpec((tq, LANE), lambda b, i: (b * tpb + i, 0)),
                  pl.BlockSpec((n_blk, n_cmp_pad), lambda b, i: (0, 0))],
        out_specs=pl.BlockSpec((tq, GROUP_WIDTH), lambda b, i: (b * tpb + i, 0)),
        scratch_shapes=[pltpu.VMEM((G, n_blk, qs), F32),
                        pltpu.VMEM((2, W * G // (2 * qs), tk // 2, 2 * qs), F32)],
        compiler_params=_cparams(("parallel", "parallel")),
        name="nsa_attn",
    )(qp, qr, kc, vc, ks, vs, kw, vw, gates, _cmp_to_slc_t(n_cmp_pad, n_blk))


def _gla_kernel(qk_ref, v_ref, g_ref, lr_ref, w2_ref, gb_ref, o_ref, state_ref, *, n_chunks):
    H, C = N_GROUP_HEADS, GLA_CHUNK
    DKW = H * GLA_DK
    DVW = H * GLA_DV

    @pl.when(pl.program_id(1) == 0)
    def _():
        state_ref[...] = jnp.zeros(state_ref.shape, F32)

    tri = (lax.broadcasted_iota(jnp.int32, (C, C), 0)
           >= lax.broadcasted_iota(jnp.int32, (C, C), 1))
    tri_b = jnp.where(tri, 1.0, 0.0).astype(BF16)
    qhead = lax.broadcasted_iota(jnp.int32, (C, DKW), 1) // GLA_DK
    vhead = lax.broadcasted_iota(jnp.int32, (C, DVW), 1) // GLA_DV
    causal = jnp.concatenate([tri] * H, axis=0)
    state_mask = (lax.broadcasted_iota(jnp.int32, (DKW, DVW), 0) // GLA_DK
                  == lax.broadcasted_iota(jnp.int32, (DKW, DVW), 1) // GLA_DV)

    pre = _dot_rhs3(lr_ref[...], w2_ref[...]) + gb_ref[...]
    log_a = (jnp.minimum(pre, 0.0) - jnp.log1p(jnp.exp(-jnp.abs(pre)))) / GLA_TAU
    q_all = qk_ref[:, 0:DKW].astype(F32) * (GLA_DK ** -0.5)
    k_all = qk_ref[:, DKW:2 * DKW].astype(F32)

    chunk_rows = [slice(ci * C, (ci + 1) * C) for ci in range(n_chunks)]
    bf = lambda a, w: jnp.dot(a, w, preferred_element_type=F32)
    la_hi, la_mid, la_lo = _split3(log_a)
    b_all = [bf(tri_b, la_hi[r]) + bf(tri_b, la_mid[r]) + bf(tri_b, la_lo[r]) for r in chunk_rows]
    v_all = [v_ref[r, :] for r in chunk_rows]
    q_stack, k_in, q_dec, k_out_t, decay = [], [], [], [], []
    for r, b in zip(chunk_rows, b_all):
        b_mid = b[C // 2:C // 2 + 1, :]
        b_last = b[C - 1:C, :]
        q, k = q_all[r], k_all[r]
        q_in = q * jnp.exp(b - b_mid)
        q_stack.append(jnp.concatenate([jnp.where(qhead == h, q_in, 0.0) for h in range(H)],
                                       axis=0).astype(BF16))
        k_in.append((k * jnp.exp(b_mid - b)).astype(BF16))
        q_dec.append((q * jnp.exp(b)).astype(BF16))
        k_out_t.append((k * jnp.exp(b_last - b)).T.astype(BF16))
        decay.append(jnp.exp(jnp.broadcast_to(b_last, (8, DKW))).T[:, 0:1])
    att = [lax.dot_general(qs_, ki_, (((1,), (1,)), ((), ())), preferred_element_type=F32)
           for qs_, ki_ in zip(q_stack, k_in)]
    att = [jnp.where(causal, a, 0.0).astype(BF16) for a in att]
    o_stack = [bf(a, v) for a, v in zip(att, v_all)]
    kv_new = [jnp.where(state_mask, bf(kt_, v), 0.0) for kt_, v in zip(k_out_t, v_all)]
    o_intra = []
    for os_ in o_stack:
        o = jnp.where(vhead == 0, os_[0:C], 0.0)
        for h in range(1, H):
            o = o + jnp.where(vhead == h, os_[h * C:(h + 1) * C], 0.0)
        o_intra.append(o)

    state = state_ref[...]
    o_chunks = []
    for ci in range(n_chunks):
        o_chunks.append(o_intra[ci] + bf(q_dec[ci], state.astype(BF16)))
        state = decay[ci] * state + kv_new[ci]
    state_ref[...] = state

    o = jnp.concatenate(o_chunks, axis=0)
    ms = _dot_lhs3(o * o, _group_mean_matrix(DVW, GLA_DV))
    gate = g_ref[...].astype(F32)
    o_ref[...] = (gate * _sigmoid(gate) * (o * lax.rsqrt(ms + RMS_EPS))).astype(o_ref.dtype)


def _gla(qk, v, g, lr, gate_w2_l, gate_b_l, batch, seq, tg):
    tpb = seq // tg
    row = lambda b, i: (b * tpb + i, 0)
    w2 = jnp.zeros((LANE, N_GROUP_HEADS * GLA_DK), F32).at[N_GATE_COLS:N_GATE_COLS + GLA_RANK].set(gate_w2_l)
    return pl.pallas_call(
        functools.partial(_gla_kernel, n_chunks=tg // GLA_CHUNK),
        out_shape=jax.ShapeDtypeStruct((batch * seq, GROUP_WIDTH), BF16),
        grid=(batch, tpb),
        in_specs=[pl.BlockSpec((tg, 2 * N_GROUP_HEADS * GLA_DK), row),
                  pl.BlockSpec((tg, GROUP_WIDTH), row),
                  pl.BlockSpec((tg, GROUP_WIDTH), row),
                  pl.BlockSpec((tg, LANE), row),
                  pl.BlockSpec((LANE, N_GROUP_HEADS * GLA_DK), lambda b, i: (0, 0)),
                  pl.BlockSpec((1, N_GROUP_HEADS * GLA_DK), lambda b, i: (0, 0))],
        out_specs=pl.BlockSpec((tg, GROUP_WIDTH), row),
        scratch_shapes=[pltpu.VMEM((N_GROUP_HEADS * GLA_DK, N_GROUP_HEADS * GLA_DV), F32)],
        compiler_params=_cparams(("parallel", "arbitrary")),
        name="gla",
    )(qk, v, g, lr, w2, gate_b_l.reshape(1, -1))


def _gmlp_kernel(uv_ref, w_ref, bias_ref, o_ref, *, n_chunks):
    T, G = GMLP_CHUNK, N_GROUP_HEADS
    gdim = GROUP_WIDTH // G
    gmean = _group_mean_matrix(GROUP_WIDTH, gdim)
    tri = (lax.broadcasted_iota(jnp.int32, (T, T), 0)
           >= lax.broadcasted_iota(jnp.int32, (T, T), 1))
    head = lax.broadcasted_iota(jnp.int32, (T, GROUP_WIDTH), 1) // gdim
    w = [jnp.where(tri, w_ref[g], 0.0).astype(BF16) for g in range(G)]
    v = uv_ref[:, GROUP_WIDTH:2 * GROUP_WIDTH]
    d = v.astype(F32) - jnp.dot(v, gmean, preferred_element_type=F32)
    vn = (d * lax.rsqrt(_dot_lhs3(d * d, gmean) + RMS_EPS)).astype(BF16)
    mixes = [[jnp.dot(w[g], vn[ci * T:(ci + 1) * T], preferred_element_type=F32) for g in range(G)]
             for ci in range(n_chunks)]
    for ci in range(n_chunks):
        rows = slice(ci * T, (ci + 1) * T)
        mixed = bias_ref[...]
        for g in range(G):
            mixed = mixed + jnp.where(head == g, mixes[ci][g], 0.0)
        o_ref[rows, :] = (uv_ref[rows, 0:GROUP_WIDTH].astype(F32) * mixed).astype(o_ref.dtype)


def _gmlp(uv, gmlp_ws_l, gmlp_b_l, rows, tc):
    gdim = GROUP_WIDTH // N_GROUP_HEADS
    bias = jnp.repeat(gmlp_b_l.T, gdim, axis=1)
    return pl.pallas_call(
        functools.partial(_gmlp_kernel, n_chunks=tc // GMLP_CHUNK),
        out_shape=jax.ShapeDtypeStruct((rows, GROUP_WIDTH), BF16),
        grid=(rows // tc,),
        in_specs=[pl.BlockSpec((tc, 2 * GROUP_WIDTH), lambda i: (i, 0)),
                  pl.BlockSpec((N_GROUP_HEADS, GMLP_CHUNK, GMLP_CHUNK), lambda i: (0, 0, 0)),
                  pl.BlockSpec((GMLP_CHUNK, GROUP_WIDTH), lambda i: (0, 0))],
        out_specs=pl.BlockSpec((tc, GROUP_WIDTH), lambda i: (i, 0)),
        compiler_params=_cparams(("parallel",)),
        name="gmlp",
    )(uv, gmlp_ws_l, bias)


def _tail_kernel(conv_ref, halo_ref, cw_ref, nsa_ref, gla_ref, gmlp_ref, gain_ref, wo_ref,
                 x_ref, mod_ref, g_ref, wu_ref, wd_ref, o_ref, *, tpb, tf):
    GW = GROUP_WIDTH
    tm = conv_ref.shape[0]
    nh = halo_ref.shape[0]
    z = conv_ref[:, 2 * GW:3 * GW].astype(F32) * conv_ref[:, 0:GW].astype(F32)
    zh = halo_ref[:, 2 * GW:3 * GW].astype(F32) * halo_ref[:, 0:GW].astype(F32)
    zh = jnp.where(pl.program_id(0) % tpb == 0, 0.0, zh)
    row = lax.broadcasted_iota(jnp.int32, (tm, GW), 0)
    y = cw_ref[CONV_WIDTH - 1:CONV_WIDTH, :] * z
    for back in range(1, CONV_WIDTH):
        shifted = pltpu.roll(z, back, axis=0)
        for r in range(back):
            shifted = jnp.where(row == r, zh[nh - back + r:nh - back + r + 1, :], shifted)
        y = y + cw_ref[CONV_WIDTH - 1 - back:CONV_WIDTH - back, :] * shifted
    conv_out = conv_ref[:, GW:2 * GW].astype(F32) * y

    halves = [slice(0, tm // 2), slice(tm // 2, tm)]
    x_mid = []
    for r in halves:
        groups = (conv_out[r], nsa_ref[r, :], gla_ref[r, :], gmlp_ref[r, :])
        acc = jnp.zeros((tm // 2, D_MODEL), F32)
        for k, o in enumerate(groups):
            n = (_rms(o.astype(F32)) * gain_ref[k:k + 1, :]).astype(BF16)
            acc = acc + jnp.dot(n, wo_ref[0, k * GW:(k + 1) * GW, :], preferred_element_type=F32)
        x_mid.append(x_ref[r, :] + mod_ref[0, 2:3, :] * (_rms(acc) * g_ref[1:2, :]))

    hb = [((_rms(x) * g_ref[2:3, :]) * (1.0 + mod_ref[0, 4:5, :]) + mod_ref[0, 3:4, :]).astype(BF16)
          for x in x_mid]
    acc = [jnp.zeros((tm // 2, D_MODEL), F32) for _ in halves]
    for f in range(D_FF // tf):
        up = [jnp.dot(h, wu_ref[0, :, f * tf:(f + 1) * tf], preferred_element_type=F32) for h in hb]
        for i, u in enumerate(up):
            u = jnp.maximum(u, 0.0)
            acc[i] = acc[i] + jnp.dot((u * u).astype(BF16), wd_ref[0, f * tf:(f + 1) * tf, :],
                                      preferred_element_type=F32)
    for r, x, a in zip(halves, x_mid, acc):
        o_ref[r, :] = x + mod_ref[0, 5:6, :] * (_rms(a) * g_ref[3:4, :])


def _layer_tail(conv, nsa, gla, gmlp, conv_w_l, grp_gain_l, wo_all, wu_all, wd_all, layer,
                x2, mod_l, norm_g_l, seq, tm):
    rows = x2.shape[0]
    tpb = seq // tm
    halo = 16
    hpt = tm // halo
    gw = lambda i: (i, 0)
    const = lambda i: (0, 0)
    weight = lambda shape: pl.BlockSpec((1,) + shape, lambda i: (layer, 0, 0),
                                        pipeline_mode=pl.Buffered(1))
    return pl.pallas_call(
        functools.partial(_tail_kernel, tpb=tpb, tf=D_MODEL),
        out_shape=jax.ShapeDtypeStruct((rows, D_MODEL), F32),
        grid=(rows // tm,),
        in_specs=[pl.BlockSpec((tm, 3 * GROUP_WIDTH), gw),
                  pl.BlockSpec((halo, 3 * GROUP_WIDTH), lambda i: (jnp.maximum(i * hpt - 1, 0), 0)),
                  pl.BlockSpec((CONV_WIDTH, GROUP_WIDTH), const),
                  pl.BlockSpec((tm, GROUP_WIDTH), gw),
                  pl.BlockSpec((tm, GROUP_WIDTH), gw),
                  pl.BlockSpec((tm, GROUP_WIDTH), gw),
                  pl.BlockSpec((N_MIXERS, GROUP_WIDTH), const),
                  weight((D_MODEL, D_MODEL)),
                  pl.BlockSpec((tm, D_MODEL), gw),
                  pl.BlockSpec((1, N_MOD, D_MODEL), lambda i: (i // tpb, 0, 0)),
                  pl.BlockSpec((4, D_MODEL), const),
                  weight((D_MODEL, D_FF)),
                  weight((D_FF, D_MODEL))],
        out_specs=pl.BlockSpec((tm, D_MODEL), gw),
        compiler_params=_cparams(("parallel",)),
        name="layer_tail",
    )(conv, conv, conv_w_l, nsa, gla, gmlp, grp_gain_l.reshape(N_MIXERS, GROUP_WIDTH), wo_all,
      x2, mod_l, norm_g_l, wu_all, wd_all)


def _pack_in_proj(w_in):
    offs = np.cumsum((0,) + (3 * GROUP_WIDTH, GROUP_WIDTH, 2 * N_BRANCH * HEAD_DIM, N_GATE_COLS,
                             2 * N_GROUP_HEADS * GLA_DK, GROUP_WIDTH, GROUP_WIDTH, GLA_RANK,
                             2 * GROUP_WIDTH))
    col = lambda a, b: w_in[..., offs[a]:offs[b]]
    narrow_pad = jnp.zeros(w_in.shape[:-1] + (LANE - N_GATE_COLS - GLA_RANK,), w_in.dtype)
    return jnp.concatenate([col(0, 3),
                            col(3, 4), col(7, 8), narrow_pad,
                            col(4, 7),
                            col(8, 9)], axis=-1).astype(BF16)


def kernel(x, c, positions, w_in, conv_w, cmp_pos, cmp_w1, cmp_w2, gla_gate_w2, gla_gate_b, gmlp_ws, gmlp_b, grp_gain, w_o, norm_g, w_mod, b_mod, w_up, w_down):
    batch, seq, _ = x.shape
    depth = w_in.shape[0]
    rows = batch * seq
    tm = min(512, seq)
    qs = min(128, seq)
    tq = min(4 * qs, seq)
    tk = min(512, seq)

    mod = _modulation(c, w_mod, b_mod).reshape(depth, batch, N_MOD, D_MODEL)
    cos_t, sin_t = _rope_tables(positions)
    x2 = x.reshape(rows, D_MODEL)
    w_in_b, w_o_b, w_up_b, w_down_b = _pack_in_proj(w_in), w_o.astype(BF16), w_up.astype(BF16), w_down.astype(BF16)
    for l in range(depth):
        conv, n_q, n_kv, narrow, l_qk, l_v, l_g, m_uv = _in_projection(
            x2, mod[l], norm_g[l], w_in_b, l, seq, tm)
        qp, qr, ks, vs, kw, vw = _nsa_prep(n_q, n_kv, cos_t, sin_t, batch, seq, tm)
        kc, vc = _nsa_compress(n_kv, cmp_pos[l], cmp_w1[l], cmp_w2[l], batch, seq)
        nsa = _nsa_attention(qp, qr, kc, vc, ks, vs, kw, vw, narrow, batch, seq, tq, qs, tk)
        gla = _gla(l_qk, l_v, l_g, narrow, gla_gate_w2[l], gla_gate_b[l], batch, seq, min(2 * tm, seq))
        gmlp = _gmlp(m_uv, gmlp_ws[l], gmlp_b[l], rows, min(2 * tm, seq))
        x2 = _layer_tail(conv, nsa, gla, gmlp, conv_w[l], grp_gain[l], w_o_b, w_up_b, w_down_b, l,
                         x2, mod[l], norm_g[l], seq, tm)
    return x2.reshape(batch, seq, D_MODEL)
```
